```python
import math, functools
import jax, jax.numpy as jnp
from jax import lax
import numpy as np

D_MODEL = 1024
BATCH = 16
SEQ = 256
DEPTH = 4
DEC_BATCH = 4
DEC_SEQ = 2048
PAST_LEN = 256

GRID_W = 64
A_WIDTH = D_MODEL // 4
B_WIDTH = D_MODEL // 4
C_WIDTH = D_MODEL // 2
MIX_WIDTH = A_WIDTH + B_WIDTH + C_WIDTH
N_HEADS_ATTN = 8
HEAD_DIM = C_WIDTH // N_HEADS_ATTN
POOL_WINDOWS = (2, 4, 8, 16)
N_POOL_GROUPS = 4
POOL_GROUP_C = B_WIDTH // N_POOL_GROUPS
NA_ROWS_MAX = 8
NA_COLS = 16
NA_QB = NA_COLS
NA_KB = 2 * NA_COLS
PEER_HEADS = 8
PEER_KEYS = 128
PEER_EXPERTS = PEER_KEYS * PEER_KEYS
PEER_TOPK = 16
PEER_QDIM = 256
PEER_HALF = PEER_QDIM // 2
PEER_CHUNK = 128
IN_COLS = 3 * A_WIDTH + B_WIDTH + 3 * C_WIDTH
IN_SPLITS = [A_WIDTH, 2 * A_WIDTH, 3 * A_WIDTH, 3 * A_WIDTH + B_WIDTH,
             3 * A_WIDTH + B_WIDTH + C_WIDTH, 3 * A_WIDTH + B_WIDTH + 2 * C_WIDTH]
N_MOD = 6
EPS = 1e-6
NEG_INF = -1e30

kernel_name = 'hybrid_conv_pool_natten_peer_diffusion_step'


def _rmsnorm(x, g):
    xf = x.astype(jnp.float32)
    y = xf * lax.rsqrt(jnp.mean(xf * xf, axis=-1, keepdims=True) + EPS)
    return (y * g.astype(jnp.float32)).astype(x.dtype)


def _modulation(cvec, w_ada_l, b_ada_l):
    m = jax.nn.silu(cvec) @ w_ada_l + b_ada_l
    return [t[:, None, :] for t in jnp.split(m, N_MOD, axis=-1)]


def _short_conv(a_in, a_b, a_c, conv_w):
    u = a_c * a_in
    L = u.shape[1]
    up = jnp.pad(u, ((0, 0), (1, 1), (0, 0)))
    y = conv_w[0] * up[:, :L] + conv_w[1] * up[:, 1:L + 1] + conv_w[2] * up[:, 2:]
    return a_b * y


def _multi_pool(p, pool_w, pool_scale):
    Bn, L, _ = p.shape
    t = np.arange(L)[:, None]
    w = np.array(POOL_WINDOWS)[None, :]
    lo = np.clip(t - w // 2, 0, L - 1)
    hi = np.clip(t + w - 1 - w // 2, 0, L - 1)
    lo_c = np.repeat(lo, POOL_GROUP_C, axis=1)
    hi_c = np.repeat(hi, POOL_GROUP_C, axis=1)
    cnt = (hi_c - lo_c + 1).astype(np.float32)
    ch = np.arange(B_WIDTH)[None, :]
    pf = p.astype(jnp.float32)
    csum = jnp.concatenate([jnp.zeros_like(pf[:, :1]), jnp.cumsum(pf, axis=1)], axis=1)
    mean = (csum[:, hi_c + 1, ch] - csum[:, lo_c, ch]) / cnt
    d = (mean - pf).astype(p.dtype).reshape(Bn, L, N_POOL_GROUPS, POOL_GROUP_C)
    y = jnp.einsum('blgc,gcd->blgd', d, pool_w).reshape(Bn, L, B_WIDTH)
    return y * pool_scale


def _heads(t):
    Bn, L, _ = t.shape
    return t.reshape(Bn, L, N_HEADS_ATTN, HEAD_DIM).transpose(0, 2, 1, 3)


def _ctx_attention(q, k, v):
    Bn, L, _ = q.shape
    qh, kh, vh = _heads(q), _heads(k), _heads(v)
    s = jnp.einsum('bhqd,bhkd->bhqk', qh, kh).astype(jnp.float32) * (HEAD_DIM ** -0.5)
    pr = jax.nn.softmax(s, axis=-1).astype(v.dtype)
    o = jnp.einsum('bhqk,bhkd->bhqd', pr, vh)
    return o.transpose(0, 2, 1, 3).reshape(Bn, L, C_WIDTH), (kh, vh)


def _na_tables(rows):
    nrow = min(NA_ROWS_MAX, rows)
    r = np.arange(rows)
    rs = np.clip(r - nrow // 2, 0, rows - nrow)
    row_idx = rs[:, None] + np.arange(nrow)[None, :]
    dr_idx = row_idx - r[:, None] + NA_ROWS_MAX - 1
    ncb = GRID_W // NA_QB
    cb0 = np.arange(ncb) * NA_QB
    kb = np.clip(cb0 - NA_COLS // 2, 0, GRID_W - NA_KB)
    col_idx = kb[:, None] + np.arange(NA_KB)[None, :]
    qcol = cb0[:, None] + np.arange(NA_QB)[None, :]
    cs = np.clip(qcol - NA_COLS // 2, 0, GRID_W - NA_COLS)
    kc = col_idx[:, None, :]
    valid = (kc >= cs[..., None]) & (kc < cs[..., None] + NA_COLS)
    dc_idx = np.clip(kc - qcol[..., None], -(NA_COLS - 1), NA_COLS - 1) + NA_COLS - 1
    return nrow, ncb, row_idx, dr_idx, col_idx, valid, dc_idx


def _latent_attention(q, k, v, ck, cv, rpb):
    Bn, N, _ = q.shape
    rows = N // GRID_W
    nrow, ncb, row_idx, dr_idx, col_idx, valid, dc_idx = _na_tables(rows)
    nw = nrow * NA_KB
    qh, kh, vh = _heads(q), _heads(k), _heads(v)
    qb = qh.reshape(Bn, N_HEADS_ATTN, rows, ncb, NA_QB, HEAD_DIM)
    ri = row_idx[:, None, :, None]
    ci = col_idx[None, :, None, :]
    kg = kh.reshape(Bn, N_HEADS_ATTN, rows, GRID_W, HEAD_DIM)[:, :, ri, ci].reshape(
        Bn, N_HEADS_ATTN, rows, ncb, nw, HEAD_DIM)
    vg = vh.reshape(Bn, N_HEADS_ATTN, rows, GRID_W, HEAD_DIM)[:, :, ri, ci].reshape(
        Bn, N_HEADS_ATTN, rows, ncb, nw, HEAD_DIM)
    scale = HEAD_DIM ** -0.5
    s_win = jnp.einsum('bhrcqd,bhrckd->bhrcqk', qb, kg).astype(jnp.float32) * scale
    bias = rpb[:, dr_idx[:, None, None, :, None], dc_idx[None, :, :, None, :]].reshape(
        N_HEADS_ATTN, rows, ncb, NA_QB, nw)
    mask = np.broadcast_to(valid[:, :, None, :], (ncb, NA_QB, nrow, NA_KB)).reshape(ncb, NA_QB, nw)
    s_win = jnp.where(mask, s_win + bias.astype(jnp.float32)[None], NEG_INF)
    Lc = ck.shape[2]
    s_ctx = (jnp.einsum('bhnd,bhmd->bhnm', qh, ck).astype(jnp.float32) * scale).reshape(
        Bn, N_HEADS_ATTN, rows, ncb, NA_QB, Lc)
    pr = jax.nn.softmax(jnp.concatenate([s_win, s_ctx], axis=-1), axis=-1).astype(v.dtype)
    o = (jnp.einsum('bhrcqk,bhrckd->bhrcqd', pr[..., :nw], vg)
         + jnp.einsum('bhrcqm,bhmd->bhrcqd', pr[..., nw:], cv))
    o = o.reshape(Bn, N_HEADS_ATTN, N, HEAD_DIM).transpose(0, 2, 1, 3).reshape(Bn, N, C_WIDTH)
    return o, None


def _peer(x, w_q, sub_keys, u, v):
    Bn, L, D = x.shape
    T = Bn * L
    xt = x.reshape(T, D)
    q = (xt @ w_q).reshape(T, PEER_HEADS, 2, PEER_HALF)
    s = jnp.einsum('thpd,hpkd->thpk', q, sub_keys).astype(jnp.float32)
    sv, si = lax.top_k(s, PEER_TOPK)
    cand = (sv[:, :, 0, :, None] + sv[:, :, 1, None, :]).reshape(T, PEER_HEADS, PEER_TOPK * PEER_TOPK)
    top, f = lax.top_k(cand, PEER_TOPK)
    i1 = jnp.take_along_axis(si[:, :, 0], f // PEER_TOPK, axis=-1)
    i2 = jnp.take_along_axis(si[:, :, 1], f % PEER_TOPK, axis=-1)
    nch = T // PEER_CHUNK
    experts = (i1 * PEER_KEYS + i2).reshape(nch, PEER_CHUNK, PEER_HEADS * PEER_TOPK)
    gates = jax.nn.softmax(top, axis=-1).astype(x.dtype).reshape(nch, PEER_CHUNK, PEER_HEADS * PEER_TOPK)

    def _chunk(args):
        xc, ec, gc = args
        h = jax.nn.gelu(jnp.einsum('tkd,td->tk', u[ec], xc))
        return jnp.einsum('tk,tkd->td', gc * h, v[ec])

    out = lax.map(_chunk, (xt.reshape(nch, PEER_CHUNK, D), experts, gates))
    return out.reshape(Bn, L, D)


def _layer(x, mods, params, attend):
    g1, g2, w_in_l, conv_l, pool_w_l, pool_s_l, w_out_l, wq_l, sk_l, u_l, v_l = params
    sh1, sc1, gt1, sh2, sc2, gt2 = mods
    h = _rmsnorm(x, g1) * (1 + sc1) + sh1
    a_in, a_b, a_c, p_in, q, k, v = jnp.split(h @ w_in_l, IN_SPLITS, axis=-1)
    y_attn, ctx_kv = attend(q, k, v)
    y = jnp.concatenate([_short_conv(a_in, a_b, a_c, conv_l),
                         _multi_pool(p_in, pool_w_l, pool_s_l),
                         y_attn], axis=-1)
    x = x + gt1 * (y @ w_out_l)
    h = _rmsnorm(x, g2) * (1 + sc2) + sh2
    x = x + gt2 * _peer(h, wq_l, sk_l, u_l, v_l)
    return x, ctx_kv


def setup_inputs(seed: int = 0) -> dict:
    key = jax.random.key(seed)
    ks = jax.random.split(key, 21)
    nrm = jax.random.normal
    f32 = jnp.float32
    D = D_MODEL
    return {
        'x_prompt': nrm(ks[0], (BATCH, SEQ, D), f32),
        'x_sample': nrm(ks[1], (DEC_BATCH, DEC_SEQ, D), f32),
        'c': nrm(ks[2], (DEC_BATCH, D), f32),
        'cache_k': nrm(ks[3], (DEC_BATCH, DEPTH, N_HEADS_ATTN, PAST_LEN, HEAD_DIM), f32),
        'cache_v': nrm(ks[4], (DEC_BATCH, DEPTH, N_HEADS_ATTN, PAST_LEN, HEAD_DIM), f32),
        'c_ctx': nrm(ks[5], (D,), f32),
        'norm1_g': 1.0 + 0.1 * nrm(ks[6], (DEPTH, D), f32),
        'norm2_g': 1.0 + 0.1 * nrm(ks[7], (DEPTH, D), f32),
        'w_ada': nrm(ks[8], (DEPTH, D, N_MOD * D), f32) * (0.5 * D ** -0.5),
        'b_ada': 0.02 * nrm(ks[9], (DEPTH, N_MOD * D), f32),
        'w_in': nrm(ks[10], (DEPTH, D, IN_COLS), f32) * D ** -0.5,
        'conv_w': nrm(ks[11], (DEPTH, 3, A_WIDTH), f32) * 3 ** -0.5,
        'pool_w': nrm(ks[12], (DEPTH, N_POOL_GROUPS, POOL_GROUP_C, POOL_GROUP_C), f32) * POOL_GROUP_C ** -0.5,
        'pool_scale': 1.0 + 0.1 * nrm(ks[13], (DEPTH, B_WIDTH), f32),
        'rpb': 0.1 * nrm(ks[14], (DEPTH, N_HEADS_ATTN, 2 * NA_ROWS_MAX - 1, 2 * NA_COLS - 1), f32),
        'w_out': nrm(ks[15], (DEPTH, MIX_WIDTH, D), f32) * MIX_WIDTH ** -0.5,
        'peer_wq': nrm(ks[16], (DEPTH, D, PEER_HEADS * PEER_QDIM), f32) * D ** -0.5,
        'peer_subkeys': nrm(ks[17], (DEPTH, PEER_HEADS, 2, PEER_KEYS, PEER_HALF), f32) * PEER_HALF ** -0.5,
        'peer_u': nrm(ks[18], (DEPTH, PEER_EXPERTS, D), f32) * D ** -0.5,
        'peer_v': nrm(ks[19], (DEPTH, PEER_EXPERTS, D), f32) * 0.5,
        'final_g': 1.0 + 0.1 * nrm(ks[20], (D,), f32),
    }


def reference(x_prompt, x_sample, c, cache_k, cache_v, c_ctx, norm1_g, norm2_g, w_ada, b_ada,
              w_in, conv_w, pool_w, pool_scale, rpb, w_out, peer_wq, peer_subkeys, peer_u, peer_v,
              final_g):
    xp, xs = x_prompt, x_sample
    cp = c_ctx[None, :]
    ks_new, vs_new = [], []
    for l in range(DEPTH):
        params = (norm1_g[l], norm2_g[l], w_in[l], conv_w[l], pool_w[l], pool_scale[l], w_out[l],
                  peer_wq[l], peer_subkeys[l], peer_u[l], peer_v[l])
        xp, (k_l, v_l) = _layer(xp, _modulation(cp, w_ada[l], b_ada[l]), params, _ctx_attention)
        ks_new.append(k_l)
        vs_new.append(v_l)
        attend = functools.partial(_latent_attention, ck=cache_k[:, l], cv=cache_v[:, l], rpb=rpb[l])
        xs, _ = _layer(xs, _modulation(c, w_ada[l], b_ada[l]), params, attend)
    y_prompt = _rmsnorm(xp, final_g)
    y_sample = _rmsnorm(xs, final_g)
    new_cache_k = jnp.stack(ks_new, axis=1)
    new_cache_v = jnp.stack(vs_new, axis=1)
    return (y_prompt, y_sample, new_cache_k, new_cache_v)
```

```python
import functools

import numpy as np
import jax
import jax.numpy as jnp
from jax import lax
from jax.experimental import pallas as pl
from jax.experimental.pallas import tpu as pltpu

F32 = jnp.float32
BF16 = jnp.bfloat16

D_MODEL = 1024
DEPTH = 4
GRID_W = 64
A_WIDTH = D_MODEL // 4
B_WIDTH = D_MODEL // 4
C_WIDTH = D_MODEL // 2
N_HEADS_ATTN = 8
HEAD_DIM = C_WIDTH // N_HEADS_ATTN
POOL_GROUP_C = B_WIDTH // 4
NA_ROWS = 8
NA_COLS = 16
PEER_HEADS = 8
PEER_KEYS = 128
PEER_EXPERTS = PEER_KEYS * PEER_KEYS
PEER_TOPK = 16
PEER_HALF = 128
IN_COLS = 3 * A_WIDTH + B_WIDTH + 3 * C_WIDTH
N_MOD = 6
EPS = 1e-6
NEG_INF = -1e30
ATTN_SCALE = HEAD_DIM ** -0.5

LANES = 128
SUBLANES = 8
MOD_ROWS = 8
VMEM_LIMIT = 56 * 1024 * 1024

TOK_BLK = 512
MIX_BLK = 2048
GATE_BLK = 128
GATE_PITCH = PEER_KEYS + SUBLANES
EXP_TOK = 1024
EXP_BLK = 1024
EXP_PLANES = EXP_BLK // PEER_KEYS


def _cparams(sem):
    return pltpu.CompilerParams(dimension_semantics=sem, vmem_limit_bytes=VMEM_LIMIT)


def _rms(x, g):
    return x * lax.rsqrt(jnp.mean(x * x, axis=-1, keepdims=True) + EPS) * g


def _mod_row(i, blk, n_ctx_tok, dec_seq):
    start = i * blk
    return jnp.where(start < n_ctx_tok, 0, 1 + jnp.maximum(start - n_ctx_tok, 0) // dec_seq)


def _mods_kernel(c_ref, w_ref, b_ref, o_ref):
    c = c_ref[...]
    s = (c * jax.nn.sigmoid(c)).astype(BF16)
    o_ref[0] = jnp.dot(s, w_ref[0].astype(BF16), preferred_element_type=F32) + b_ref[0]


def _mods_call(cvec, w_ada, b_ada):
    depth, d, n = w_ada.shape
    tn = 1536
    return pl.pallas_call(
        _mods_kernel,
        grid=(depth, n // tn),
        in_specs=[
            pl.BlockSpec((MOD_ROWS, d), lambda l, j: (0, 0)),
            pl.BlockSpec((1, d, tn), lambda l, j: (l, 0, j)),
            pl.BlockSpec((1, 1, tn), lambda l, j: (l, 0, j)),
        ],
        out_specs=pl.BlockSpec((1, MOD_ROWS, tn), lambda l, j: (l, 0, j)),
        out_shape=jax.ShapeDtypeStruct((depth, MOD_ROWS, n), F32),
        compiler_params=_cparams(("arbitrary", "arbitrary")),
        name="mods",
    )(cvec, w_ada, b_ada.reshape(depth, 1, n))


def _w2_kernel(k_ref, wq_ref, o_ref):
    o_ref[0, 0, 0] = lax.dot_general(
        k_ref[0, 0, 0], wq_ref[0], (((1,), (1,)), ((), ())),
        precision=lax.Precision.HIGHEST, preferred_element_type=F32).astype(BF16)


def _w2_call(peer_wq, peer_subkeys):
    depth, d, _ = peer_wq.shape
    return pl.pallas_call(
        _w2_kernel,
        grid=(depth, 2, PEER_HEADS),
        in_specs=[
            pl.BlockSpec((1, 1, 1, PEER_KEYS, PEER_HALF), lambda l, p, h: (l, h, p, 0, 0)),
            pl.BlockSpec((1, d, PEER_HALF), lambda l, p, h: (l, 0, 2 * h + p)),
        ],
        out_specs=pl.BlockSpec((1, 1, 1, PEER_KEYS, d), lambda l, p, h: (l, p, h, 0, 0)),
        out_shape=jax.ShapeDtypeStruct((depth, 2, PEER_HEADS, PEER_KEYS, d), BF16),
        compiler_params=_cparams(("arbitrary",) * 3),
        name="peer_w2",
    )(peer_subkeys, peer_wq)


def _bias_kernel(rpb_ref, o_ref):
    l = pl.program_id(0)
    h = pl.program_id(1)
    n_dr = 2 * NA_ROWS - 1
    n_dc = 2 * NA_COLS - 1
    base = (l * N_HEADS_ATTN + h) * (n_dr * n_dc)
    c = lax.broadcasted_iota(jnp.int32, (GRID_W, LANES), 0)
    lane = lax.broadcasted_iota(jnp.int32, (GRID_W, LANES), 1)
    kc = lane & (GRID_W - 1)
    second = lane >= GRID_W
    diff = kc - c + (NA_COLS - 1)
    cs = jnp.clip(c - NA_COLS // 2, 0, GRID_W - NA_COLS)
    valid = (kc >= cs) & (kc < cs + NA_COLS)
    tiles = []
    for dr in range(n_dr - 1):
        t = jnp.zeros((GRID_W, LANES), F32)
        for dc in range(n_dc):
            hit = diff == dc
            t = jnp.where(hit & ~second, rpb_ref[base + dr * n_dc + dc], t)
            t = jnp.where(hit & second, rpb_ref[base + (dr + 1) * n_dc + dc], t)
        tiles.append(jnp.where(valid, t, NEG_INF))
    for v in range(NA_ROWS):
        for m in range(NA_ROWS // 2):
            o_ref[0, 0, v, :, m * LANES:(m + 1) * LANES] = tiles[2 * m - v + NA_ROWS - 1]


def _bias_call(rpb):
    depth = rpb.shape[0]
    return pl.pallas_call(
        _bias_kernel,
        grid=(depth, N_HEADS_ATTN),
        in_specs=[pl.BlockSpec(memory_space=pltpu.SMEM)],
        out_specs=pl.BlockSpec((1, 1, NA_ROWS, GRID_W, NA_ROWS * GRID_W), lambda l, h: (l, h, 0, 0, 0)),
        out_shape=jax.ShapeDtypeStruct((depth, N_HEADS_ATTN, NA_ROWS, GRID_W, NA_ROWS * GRID_W), F32),
        compiler_params=_cparams(("arbitrary", "arbitrary")),
        name="na_bias",
    )(rpb.reshape(-1))


def _in_kernel(x_ref, m_ref, g_ref, w_ref, o_ref, *, n_ctx_tok, dec_seq):
    d = D_MODEL
    row = _mod_row(pl.program_id(0), x_ref.shape[0], n_ctx_tok, dec_seq)
    sh = m_ref[0, pl.ds(row, 1), 0:d]
    sc = m_ref[0, pl.ds(row, 1), d:2 * d]
    h = _rms(x_ref[...], g_ref[0]) * (1.0 + sc) + sh
    o_ref[...] = jnp.dot(h.astype(BF16), w_ref[0], preferred_element_type=F32)


def _in_call(x, mods, g1, w_in, l, n_ctx_tok, dec_seq):
    t, d = x.shape
    n = w_in.shape[-1]
    return pl.pallas_call(
        functools.partial(_in_kernel, n_ctx_tok=n_ctx_tok, dec_seq=dec_seq),
        grid=(t // TOK_BLK,),
        in_specs=[
            pl.BlockSpec((TOK_BLK, d), lambda i: (i, 0)),
            pl.BlockSpec((1, MOD_ROWS, N_MOD * d), lambda i: (l, 0, 0)),
            pl.BlockSpec((1, 1, d), lambda i: (l, 0, 0)),
            pl.BlockSpec((1, d, n), lambda i: (l, 0, 0)),
        ],
        out_specs=pl.BlockSpec((TOK_BLK, n), lambda i: (i, 0)),
        out_shape=jax.ShapeDtypeStruct((t, n), F32),
        compiler_params=_cparams(("arbitrary",)),
        name="in_proj",
    )(x, mods, g1, w_in)


def _mix_kernel(p_ref, cw_ref, bd_ref, ps_ref, o_ref, *, n_ctx_blk, seq_c, seq_l):
    blk = p_ref.shape[0]
    lseq = jnp.where(pl.program_id(0) < n_ctx_blk, seq_c, seq_l)
    rows = lax.broadcasted_iota(jnp.int32, (blk, A_WIDTH), 0)
    lane = lax.broadcasted_iota(jnp.int32, (blk, A_WIDTH), 1)
    tpos = rows & (lseq - 1)

    def shift_dn(x, m):
        return jnp.where(tpos >= m, pltpu.roll(x, m, 0), 0.0)

    def shift_up(x, m):
        return jnp.where(tpos + m < lseq, pltpu.roll(x, blk - m, 0), 0.0)

    a_in = p_ref[:, 0:A_WIDTH]
    a_b = p_ref[:, A_WIDTH:2 * A_WIDTH]
    a_c = p_ref[:, 2 * A_WIDTH:3 * A_WIDTH]
    u = a_c * a_in
    cw = cw_ref[0]
    y = cw[0:1] * shift_dn(u, 1) + cw[1:2] * u + cw[2:3] * shift_up(u, 1)
    o_ref[:, 0:A_WIDTH] = a_b * y

    p = p_ref[:, 3 * A_WIDTH:3 * A_WIDTH + B_WIDTH]
    f1 = p
    f2 = f1 + shift_up(f1, 1)
    f4 = f2 + shift_up(f2, 2)
    f8 = f4 + shift_up(f4, 4)
    b1 = shift_dn(p, 1)
    b2 = b1 + shift_dn(b1, 1)
    b4 = b2 + shift_dn(b2, 2)
    b8 = b4 + shift_dn(b4, 4)
    grp = jnp.right_shift(lane, POOL_GROUP_C.bit_length() - 1)
    wsum = jnp.where(grp == 0, b1 + f1, jnp.where(grp == 1, b2 + f2, jnp.where(grp == 2, b4 + f4, b8 + f8)))
    half = jnp.left_shift(1, grp)
    lo = jnp.maximum(tpos - half, 0)
    hi = jnp.minimum(tpos + half - 1, lseq - 1)
    cnt = (hi - lo + 1).astype(F32)
    dlt = wsum / cnt - p
    o_ref[:, A_WIDTH:A_WIDTH + B_WIDTH] = (
        jnp.dot(dlt.astype(BF16), bd_ref[0], preferred_element_type=F32) * ps_ref[0])


def _mix_call(proj, conv_w, pool_bd, pool_scale, l, n_ctx_tok, seq_c, seq_l):
    t = proj.shape[0]
    return pl.pallas_call(
        functools.partial(_mix_kernel, n_ctx_blk=n_ctx_tok // MIX_BLK, seq_c=seq_c, seq_l=seq_l),
        grid=(t // MIX_BLK,),
        in_specs=[
            pl.BlockSpec((MIX_BLK, 3 * A_WIDTH + B_WIDTH), lambda i: (i, 0)),
            pl.BlockSpec((1, 3, A_WIDTH), lambda i: (l, 0, 0)),
            pl.BlockSpec((1, B_WIDTH, B_WIDTH), lambda i: (l, 0, 0)),
            pl.BlockSpec((1, 1, B_WIDTH), lambda i: (l, 0, 0)),
        ],
        out_specs=pl.BlockSpec((MIX_BLK, A_WIDTH + B_WIDTH), lambda i: (i, 0)),
        out_shape=jax.ShapeDtypeStruct((t, A_WIDTH + B_WIDTH), F32),
        compiler_params=_cparams(("arbitrary",)),
        name="conv_pool",
    )(proj, conv_w, pool_bd, pool_scale)


def _attn_ctx_kernel(q_ref, k_ref, v_ref, y_ref, ko_ref, vo_ref):
    q = q_ref[...]
    k = k_ref[...]
    v = v_ref[...]
    lane = lax.broadcasted_iota(jnp.int32, q.shape, 1)
    kb = k.astype(BF16)
    vb = v.astype(BF16)
    outs = []
    for h in range(2):
        qh = jnp.where((lane >= HEAD_DIM) == (h == 1), q, 0.0).astype(BF16)
        s = lax.dot_general(qh, kb, (((1,), (1,)), ((), ())), preferred_element_type=F32) * ATTN_SCALE
        e = jnp.exp(s - jnp.max(s, axis=-1, keepdims=True))
        pr = e / jnp.sum(e, axis=-1, keepdims=True)
        outs.append(jnp.dot(pr.astype(BF16), vb, preferred_element_type=F32))
        ko_ref[0, h] = k[:, h * HEAD_DIM:(h + 1) * HEAD_DIM]
        vo_ref[0, h] = v[:, h * HEAD_DIM:(h + 1) * HEAD_DIM]
    y_ref[...] = jnp.where(lane < HEAD_DIM, outs[0], outs[1])


def _attn_ctx_call(proj, n_batch, seq):
    col0 = (3 * A_WIDTH + B_WIDTH) // LANES
    ncol = C_WIDTH // LANES
    kv_shape = jax.ShapeDtypeStruct((n_batch, N_HEADS_ATTN, seq, HEAD_DIM), F32)
    kv_spec = pl.BlockSpec((1, 2, seq, HEAD_DIM), lambda b, hp: (b, hp, 0, 0))
    return pl.pallas_call(
        _attn_ctx_kernel,
        grid=(n_batch, ncol),
        in_specs=[
            pl.BlockSpec((seq, LANES), lambda b, hp: (b, col0 + hp)),
            pl.BlockSpec((seq, LANES), lambda b, hp: (b, col0 + ncol + hp)),
            pl.BlockSpec((seq, LANES), lambda b, hp: (b, col0 + 2 * ncol + hp)),
        ],
        out_specs=[pl.BlockSpec((seq, LANES), lambda b, hp: (b, hp)), kv_spec, kv_spec],
        out_shape=[jax.ShapeDtypeStruct((n_batch * seq, C_WIDTH), F32), kv_shape, kv_shape],
        compiler_params=_cparams(("arbitrary", "arbitrary")),
        name="attn_ctx",
    )(proj, proj, proj)


def _attn_lat_kernel(q_ref, k_ref, v_ref, ck_ref, cv_ref, bv_ref, y_ref):
    n = q_ref.shape[0]
    rows = n // GRID_W
    nrow = min(NA_ROWS, rows)
    nwin = nrow * GRID_W
    lane = lax.broadcasted_iota(jnp.int32, (GRID_W, LANES), 1)
    ck = ck_ref[0, 0].astype(BF16)
    cv = cv_ref[0, 0].astype(BF16)
    dn = (((1,), (1,)), ((), ()))

    def body(r, carry):
        rs = jnp.clip(r - nrow // 2, 0, rows - nrow)
        q = q_ref[pl.ds(pl.multiple_of(r * GRID_W, GRID_W), GRID_W), :]
        koff = pl.multiple_of(rs * GRID_W, GRID_W)
        kw = k_ref[pl.ds(koff, nwin), :].astype(BF16)
        vw = v_ref[pl.ds(koff, nwin), :].astype(BF16)
        outs = []
        for h in range(2):
            qh = jnp.where((lane >= HEAD_DIM) == (h == 1), q, 0.0).astype(BF16)
            s_w = lax.dot_general(qh, kw, dn, preferred_element_type=F32) * ATTN_SCALE + bv_ref[0, h, r - rs]
            s_c = lax.dot_general(qh, ck, dn, preferred_element_type=F32) * ATTN_SCALE
            mx = jnp.maximum(jnp.max(s_w, axis=-1, keepdims=True), jnp.max(s_c, axis=-1, keepdims=True))
            e_w = jnp.exp(s_w - mx)
            e_c = jnp.exp(s_c - mx)
            z = jnp.sum(e_w, axis=-1, keepdims=True) + jnp.sum(e_c, axis=-1, keepdims=True)
            o = (jnp.dot(e_w.astype(BF16), vw, preferred_element_type=F32)
                 + jnp.dot(e_c.astype(BF16), cv, preferred_element_type=F32))
            outs.append(o / z)
        y_ref[pl.ds(pl.multiple_of(r * GRID_W, GRID_W), GRID_W), :] = jnp.where(lane < HEAD_DIM, outs[0], outs[1])
        return carry

    lax.fori_loop(0, rows, body, 0)


def _attn_lat_call(proj, ck, cv, bias, l, n_ctx_tok, n_batch, seq):
    col0 = (3 * A_WIDTH + B_WIDTH) // LANES
    ncol = C_WIDTH // LANES
    row0 = n_ctx_tok // seq
    past = ck.shape[2]
    return pl.pallas_call(
        _attn_lat_kernel,
        grid=(n_batch, ncol),
        in_specs=[
            pl.BlockSpec((seq, LANES), lambda b, hp: (row0 + b, col0 + hp)),
            pl.BlockSpec((seq, LANES), lambda b, hp: (row0 + b, col0 + ncol + hp)),
            pl.BlockSpec((seq, LANES), lambda b, hp: (row0 + b, col0 + 2 * ncol + hp)),
            pl.BlockSpec((1, 1, past, LANES), lambda b, hp: (b, l, 0, hp)),
            pl.BlockSpec((1, 1, past, LANES), lambda b, hp: (b, l, 0, hp)),
            pl.BlockSpec((1, 2, NA_ROWS, GRID_W, NA_ROWS * GRID_W), lambda b, hp: (l, hp, 0, 0, 0)),
        ],
        out_specs=pl.BlockSpec((seq, LANES), lambda b, hp: (b, hp)),
        out_shape=jax.ShapeDtypeStruct((n_batch * seq, C_WIDTH), F32),
        compiler_params=_cparams(("arbitrary", "arbitrary")),
        name="attn_lat",
    )(proj, proj, proj, ck, cv, bias)


def _out_kernel(ycp_ref, yat_ref, x_ref, m_ref, g_ref, w_ref, xo_ref, h_ref, *, n_ctx_tok, dec_seq):
    d = D_MODEL
    half = A_WIDTH + B_WIDTH
    row = _mod_row(pl.program_id(0), x_ref.shape[0], n_ctx_tok, dec_seq)
    y = (jnp.dot(ycp_ref[...].astype(BF16), w_ref[0, 0:half], preferred_element_type=F32)
         + jnp.dot(yat_ref[...].astype(BF16), w_ref[0, half:2 * half], preferred_element_type=F32))
    x = x_ref[...] + m_ref[0, pl.ds(row, 1), 2 * d:3 * d] * y
    xo_ref[...] = x
    sh = m_ref[0, pl.ds(row, 1), 3 * d:4 * d]
    sc = m_ref[0, pl.ds(row, 1), 4 * d:5 * d]
    h_ref[...] = (_rms(x, g_ref[0]) * (1.0 + sc) + sh).astype(BF16)


def _out_call(ycp, yat, x, mods, g2, w_out, l, n_ctx_tok, dec_seq):
    t, d = x.shape
    half = ycp.shape[1]
    return pl.pallas_call(
        functools.partial(_out_kernel, n_ctx_tok=n_ctx_tok, dec_seq=dec_seq),
        grid=(t // TOK_BLK,),
        in_specs=[
            pl.BlockSpec((TOK_BLK, half), lambda i: (i, 0)),
            pl.BlockSpec((TOK_BLK, half), lambda i: (i, 0)),
            pl.BlockSpec((TOK_BLK, d), lambda i: (i, 0)),
            pl.BlockSpec((1, MOD_ROWS, N_MOD * d), lambda i: (l, 0, 0)),
            pl.BlockSpec((1, 1, d), lambda i: (l, 0, 0)),
            pl.BlockSpec((1, d, d), lambda i: (l, 0, 0)),
        ],
        out_specs=[pl.BlockSpec((TOK_BLK, d), lambda i: (i, 0)), pl.BlockSpec((TOK_BLK, d), lambda i: (i, 0))],
        out_shape=[jax.ShapeDtypeStruct((t, d), F32), jax.ShapeDtypeStruct((t, d), BF16)],
        compiler_params=_cparams(("arbitrary",)),
        name="out_proj",
    )(ycp, yat, x, mods, g2, w_out)


def _oem_sort_pairs(n):
    pairs = []

    def merge(lo, m, r):
        step = r * 2
        if step < m:
            merge(lo, m, step)
            merge(lo + r, m, step)
            pairs.extend((i, i + r) for i in range(lo + r, lo + m - r, step))
        else:
            pairs.append((lo, lo + r))

    def sort(lo, m):
        if m > 1:
            sort(lo, m // 2)
            sort(lo + m // 2, m // 2)
            merge(lo, m, 1)

    sort(0, n)
    return pairs


_SORT16 = _oem_sort_pairs(PEER_TOPK)
_CAND = [(a, b) for a in range(PEER_TOPK) for b in range(PEER_TOPK) if (a + 1) * (b + 1) <= PEER_TOPK]


def _cmp_exchange(vals, idxs, i, j):
    first = vals[i] >= vals[j]
    vals[i], vals[j] = jnp.maximum(vals[i], vals[j]), jnp.minimum(vals[i], vals[j])
    idxs[i], idxs[j] = jnp.where(first, idxs[i], idxs[j]), jnp.where(first, idxs[j], idxs[i])


def _top16(load):
    k = PEER_TOPK
    groups = []
    for g in range(PEER_KEYS // k):
        vals = [load(g * k + i) for i in range(k)]
        idxs = [jnp.full(vals[0].shape, float(g * k + i), F32) for i in range(k)]
        for i, j in _SORT16:
            _cmp_exchange(vals, idxs, i, j)
        groups.append((vals, idxs))
    while len(groups) > 1:
        merged = []
        for (va, ia), (vb, ib) in zip(groups[0::2], groups[1::2]):
            vals, idxs = [], []
            for i in range(k):
                first = va[i] >= vb[k - 1 - i]
                vals.append(jnp.maximum(va[i], vb[k - 1 - i]))
                idxs.append(jnp.where(first, ia[i], ib[k - 1 - i]))
            stride = k // 2
            while stride >= 1:
                for i in range(k):
                    if i & stride == 0:
                        _cmp_exchange(vals, idxs, i, i + stride)
                stride //= 2
            merged.append((vals, idxs))
        groups = merged
    return groups[0]


def _gate_kernel(h_ref, w2_ref, o_ref, sc_ref, i1_ref, i2_ref, g_ref, gs_ref):
    nk = PEER_KEYS
    k = PEER_TOPK
    sc_ref[...] = lax.dot_general(w2_ref[0], h_ref[...], (((1,), (1,)), ((), ())), preferred_element_type=F32)
    tops = [_top16(lambda key, p=p: sc_ref[pl.ds((p * nk + key) * PEER_HEADS, PEER_HEADS), :]) for p in range(2)]
    (v1, x1), (v2, x2) = tops
    cand = [v1[a] + v2[b] for a, b in _CAND]
    slot_v, slot_1, slot_2 = [], [], []
    for _ in range(k):
        mx = functools.reduce(jnp.maximum, cand)
        found = jnp.zeros(mx.shape, jnp.bool_)
        s1 = jnp.zeros(mx.shape, F32)
        s2 = jnp.zeros(mx.shape, F32)
        for n, (a, b) in enumerate(_CAND):
            eq = cand[n] == mx
            sel = eq & ~found
            found = found | eq
            s1 = jnp.where(sel, x1[a], s1)
            s2 = jnp.where(sel, x2[b], s2)
            cand[n] = jnp.where(sel, -jnp.inf, cand[n])
        slot_v.append(mx)
        slot_1.append(s1)
        slot_2.append(s2)
    e = [jnp.exp(v - slot_v[0]) for v in slot_v]
    z = functools.reduce(jnp.add, e)
    gates = [x / z for x in e]
    i1_ref[...] = jnp.concatenate(slot_1, axis=0).T
    i2_ref[...] = jnp.concatenate(slot_2, axis=0).T
    g_ref[...] = jnp.concatenate(gates, axis=0).T
    key_iota = lax.broadcasted_iota(jnp.int32, (nk, nk), 0).astype(F32)

    def per_token(t, carry):
        r1 = i1_ref[pl.ds(t, 1), :]
        r2 = i2_ref[pl.ds(t, 1), :]
        rg = g_ref[pl.ds(t, 1), :]
        lt = jnp.where(key_iota == r1, 1.0, 0.0).astype(BF16)
        rt = jnp.where(key_iota == r2, rg, 0.0).astype(BF16)
        gt = lax.dot_general(lt, rt, (((1,), (1,)), ((), ())), preferred_element_type=F32)
        gs_ref[pl.ds(pl.multiple_of(t * GATE_PITCH, SUBLANES), nk), :] = gt
        return carry

    lax.fori_loop(0, GATE_BLK, per_token, 0)
    for i1 in range(nk):
        o_ref[i1] = gs_ref[pl.ds(i1, GATE_BLK, stride=GATE_PITCH), :].astype(BF16)


def _gate_call(h2, w2t, l):
    t, d = h2.shape
    nrow = w2t.shape[1]
    return pl.pallas_call(
        _gate_kernel,
        grid=(t // GATE_BLK,),
        in_specs=[
            pl.BlockSpec((GATE_BLK, d), lambda i: (i, 0)),
            pl.BlockSpec((1, nrow, d), lambda i: (l, 0, 0)),
        ],
        out_specs=pl.BlockSpec((PEER_KEYS, GATE_BLK, PEER_KEYS), lambda i: (0, i, 0)),
        out_shape=jax.ShapeDtypeStruct((PEER_KEYS, t, PEER_KEYS), BF16),
        scratch_shapes=[
            pltpu.VMEM((nrow, GATE_BLK), F32),
            pltpu.VMEM((GATE_BLK, PEER_HEADS * PEER_TOPK), F32),
            pltpu.VMEM((GATE_BLK, PEER_HEADS * PEER_TOPK), F32),
            pltpu.VMEM((GATE_BLK, PEER_HEADS * PEER_TOPK), F32),
            pltpu.VMEM((GATE_BLK * GATE_PITCH, PEER_KEYS), F32),
        ],
        compiler_params=_cparams(("arbitrary",)),
        name="peer_gates",
    )(h2, w2t)


def _gelu(x):
    return 0.5 * x * (1.0 + jnp.tanh(0.7978845608028654 * (x + 0.044715 * (x * x * x))))


def _expert_kernel(h_ref, u_ref, v_ref, g_ref, x_ref, m_ref, o_ref, acc_ref, *, n_ctx_tok, dec_seq):
    d = D_MODEL
    j = pl.program_id(1)

    @pl.when(j == 0)
    def _():
        acc_ref[...] = jnp.zeros_like(acc_ref)

    a = lax.dot_general(h_ref[...], u_ref[0], (((1,), (1,)), ((), ())), preferred_element_type=F32)
    hid = jnp.concatenate(
        [(_gelu(a[:, q * PEER_KEYS:(q + 1) * PEER_KEYS]) * g_ref[q].astype(F32)).astype(BF16)
         for q in range(EXP_PLANES)], axis=1)
    acc_ref[...] += jnp.dot(hid, v_ref[0], preferred_element_type=F32)

    @pl.when(j == pl.num_programs(1) - 1)
    def _():
        row = _mod_row(pl.program_id(0), x_ref.shape[0], n_ctx_tok, dec_seq)
        o_ref[...] = x_ref[...] + m_ref[0, pl.ds(row, 1), 5 * d:6 * d] * acc_ref[...]


def _expert_call(h2, u, v, gates, x, mods, l, n_ctx_tok, dec_seq):
    t, d = x.shape
    ne = u.shape[1]
    return pl.pallas_call(
        functools.partial(_expert_kernel, n_ctx_tok=n_ctx_tok, dec_seq=dec_seq),
        grid=(t // EXP_TOK, ne // EXP_BLK),
        in_specs=[
            pl.BlockSpec((EXP_TOK, d), lambda i, j: (i, 0)),
            pl.BlockSpec((1, EXP_BLK, d), lambda i, j: (l, j, 0)),
            pl.BlockSpec((1, EXP_BLK, d), lambda i, j: (l, j, 0)),
            pl.BlockSpec((EXP_PLANES, EXP_TOK, PEER_KEYS), lambda i, j: (j, i, 0)),
            pl.BlockSpec((EXP_TOK, d), lambda i, j: (i, 0)),
            pl.BlockSpec((1, MOD_ROWS, N_MOD * d), lambda i, j: (l, 0, 0)),
        ],
        out_specs=pl.BlockSpec((EXP_TOK, d), lambda i, j: (i, 0)),
        out_shape=jax.ShapeDtypeStruct((t, d), F32),
        scratch_shapes=[pltpu.VMEM((EXP_TOK, d), F32)],
        compiler_params=_cparams(("arbitrary", "arbitrary")),
        name="peer_experts",
    )(h2, u, v, gates, x, mods)


def _final_kernel(x_ref, g_ref, o_ref):
    o_ref[...] = _rms(x_ref[...], g_ref[...])


def _final_call(x, g):
    t, d = x.shape
    return pl.pallas_call(
        _final_kernel,
        grid=(t // TOK_BLK,),
        in_specs=[pl.BlockSpec((TOK_BLK, d), lambda i: (i, 0)), pl.BlockSpec((1, d), lambda i: (0, 0))],
        out_specs=pl.BlockSpec((TOK_BLK, d), lambda i: (i, 0)),
        out_shape=jax.ShapeDtypeStruct((t, d), F32),
        compiler_params=_cparams(("arbitrary",)),
        name="final_norm",
    )(x, g.reshape(1, d))


def kernel(x_prompt, x_sample, c, cache_k, cache_v, c_ctx, norm1_g, norm2_g, w_ada, b_ada, w_in, conv_w,
           pool_w, pool_scale, rpb, w_out, peer_wq, peer_subkeys, peer_u, peer_v, final_g):
    n_b, seq_c, d = x_prompt.shape
    n_db, seq_l, _ = x_sample.shape
    depth = w_ada.shape[0]
    n_ctx_tok = n_b * seq_c
    assert d == D_MODEL and seq_l % GRID_W == 0 and seq_l // GRID_W >= NA_ROWS
    assert n_ctx_tok % MIX_BLK == 0 and seq_l % MIX_BLK == 0 and MIX_BLK % seq_c == 0
    assert 1 + n_db <= MOD_ROWS and (n_ctx_tok + n_db * seq_l) % EXP_TOK == 0

    x = jnp.concatenate([x_prompt.reshape(n_ctx_tok, d), x_sample.reshape(n_db * seq_l, d)], axis=0)
    cvec = jnp.concatenate([c_ctx[None, :], c, jnp.zeros((MOD_ROWS - 1 - n_db, d), F32)], axis=0)
    mods = _mods_call(cvec, w_ada, b_ada)
    w2t = _w2_call(peer_wq, peer_subkeys)
    w2t = w2t.transpose(0, 1, 3, 2, 4).reshape(depth, 2 * PEER_KEYS * PEER_HEADS, d)
    bias = _bias_call(rpb)
    past = cache_k.shape[3]
    ck = cache_k.transpose(0, 1, 3, 2, 4).reshape(n_db, depth, past, C_WIDTH)
    cv = cache_v.transpose(0, 1, 3, 2, 4).reshape(n_db, depth, past, C_WIDTH)
    w_in_b = w_in.astype(BF16)
    w_out_b = w_out.astype(BF16)
    u_b = peer_u.astype(BF16)
    v_b = peer_v.astype(BF16)
    eye = jnp.eye(B_WIDTH // POOL_GROUP_C, dtype=F32)
    pool_bd = jnp.einsum('gh,lgcd->lgchd', eye, pool_w).reshape(depth, B_WIDTH, B_WIDTH).astype(BF16)
    g1 = norm1_g.reshape(depth, 1, d)
    g2 = norm2_g.reshape(depth, 1, d)
    ps = pool_scale.reshape(depth, 1, B_WIDTH)

    ks_new, vs_new = [], []
    for l in range(depth):
        proj = _in_call(x, mods, g1, w_in_b, l, n_ctx_tok, seq_l)
        ycp = _mix_call(proj, conv_w, pool_bd, ps, l, n_ctx_tok, seq_c, seq_l)
        y_c, k_l, v_l = _attn_ctx_call(proj, n_b, seq_c)
        y_l = _attn_lat_call(proj, ck, cv, bias, l, n_ctx_tok, n_db, seq_l)
        ks_new.append(k_l)
        vs_new.append(v_l)
        yat = jnp.concatenate([y_c, y_l], axis=0)
        x, h2 = _out_call(ycp, yat, x, mods, g2, w_out_b, l, n_ctx_tok, seq_l)
        gates = _gate_call(h2, w2t, l)
        x = _expert_call(h2, u_b, v_b, gates, x, mods, l, n_ctx_tok, seq_l)
    y = _final_call(x, final_g)
    y_prompt = y[:n_ctx_tok].reshape(n_b, seq_c, d)
    y_sample = y[n_ctx_tok:].reshape(n_db, seq_l, d)
    return (y_prompt, y_sample, jnp.stack(ks_new, axis=1), jnp.stack(vs_new, axis=1))
```

```python
import functools

import numpy as np
import jax
import jax.numpy as jnp
from jax import lax
from jax.experimental import pallas as pl
from jax.experimental.pallas import tpu as pltpu

F32 = jnp.float32
BF16 = jnp.bfloat16

D_MODEL = 1024
DEPTH = 4
GRID_W = 64
A_WIDTH = D_MODEL // 4
B_WIDTH = D_MODEL // 4
C_WIDTH = D_MODEL // 2
N_HEADS_ATTN = 8
HEAD_DIM = C_WIDTH // N_HEADS_ATTN
POOL_GROUP_C = B_WIDTH // 4
NA_ROWS = 8
NA_COLS = 16
PEER_HEADS = 8
PEER_KEYS = 128
PEER_EXPERTS = PEER_KEYS * PEER_KEYS
PEER_TOPK = 16
PEER_HALF = 128
IN_COLS = 3 * A_WIDTH + B_WIDTH + 3 * C_WIDTH
N_MOD = 6
EPS = 1e-6
NEG_INF = -1e30
ATTN_SCALE = HEAD_DIM ** -0.5

LANES = 128
SUBLANES = 8
MOD_ROWS = 8
VMEM_LIMIT = 56 * 1024 * 1024

TOK_BLK = 512
MIX_BLK = 2048
GATE_BLK = 128
GATE_PITCH = PEER_KEYS + SUBLANES
GATE_UNROLL = 8
ATTN_UNROLL = 4
EXP_TOK = 1024
EXP_BLK = 1024
EXP_PLANES = EXP_BLK // PEER_KEYS


def _cparams(sem):
    return pltpu.CompilerParams(dimension_semantics=sem, vmem_limit_bytes=VMEM_LIMIT)


def _rms(x, g):
    return x * lax.rsqrt(jnp.mean(x * x, axis=-1, keepdims=True) + EPS) * g


def _mod_row(i, blk, n_ctx_tok, dec_seq):
    start = i * blk
    return jnp.where(start < n_ctx_tok, 0, 1 + jnp.maximum(start - n_ctx_tok, 0) // dec_seq)


def _mods_kernel(c_ref, w_ref, b_ref, o_ref):
    c = c_ref[...]
    s = (c * jax.nn.sigmoid(c)).astype(BF16)
    o_ref[0] = jnp.dot(s, w_ref[0].astype(BF16), preferred_element_type=F32) + b_ref[0]


def _mods_call(cvec, w_ada, b_ada):
    depth, d, n = w_ada.shape
    tn = 1536
    return pl.pallas_call(
        _mods_kernel,
        grid=(depth, n // tn),
        in_specs=[
            pl.BlockSpec((MOD_ROWS, d), lambda l, j: (0, 0)),
            pl.BlockSpec((1, d, tn), lambda l, j: (l, 0, j)),
            pl.BlockSpec((1, 1, tn), lambda l, j: (l, 0, j)),
        ],
        out_specs=pl.BlockSpec((1, MOD_ROWS, tn), lambda l, j: (l, 0, j)),
        out_shape=jax.ShapeDtypeStruct((depth, MOD_ROWS, n), F32),
        compiler_params=_cparams(("arbitrary", "arbitrary")),
        name="mods",
    )(cvec, w_ada, b_ada.reshape(depth, 1, n))


def _w2_kernel(k_ref, wq_ref, o_ref):
    o_ref[0, 0, 0] = lax.dot_general(
        k_ref[0, 0, 0], wq_ref[0], (((1,), (1,)), ((), ())),
        precision=lax.Precision.HIGHEST, preferred_element_type=F32).astype(BF16)


def _w2_call(peer_wq, peer_subkeys):
    depth, d, _ = peer_wq.shape
    return pl.pallas_call(
        _w2_kernel,
        grid=(depth, 2, PEER_HEADS),
        in_specs=[
            pl.BlockSpec((1, 1, 1, PEER_KEYS, PEER_HALF), lambda l, p, h: (l, h, p, 0, 0)),
            pl.BlockSpec((1, d, PEER_HALF), lambda l, p, h: (l, 0, 2 * h + p)),
        ],
        out_specs=pl.BlockSpec((1, 1, 1, PEER_KEYS, d), lambda l, p, h: (l, p, h, 0, 0)),
        out_shape=jax.ShapeDtypeStruct((depth, 2, PEER_HEADS, PEER_KEYS, d), BF16),
        compiler_params=_cparams(("arbitrary",) * 3),
        name="peer_w2",
    )(peer_subkeys, peer_wq)


def _bias_kernel(rpb_ref, o_ref):
    l = pl.program_id(0)
    h = pl.program_id(1)
    n_dr = 2 * NA_ROWS - 1
    n_dc = 2 * NA_COLS - 1
    base = (l * N_HEADS_ATTN + h) * (n_dr * n_dc)
    c = lax.broadcasted_iota(jnp.int32, (GRID_W, LANES), 0)
    lane = lax.broadcasted_iota(jnp.int32, (GRID_W, LANES), 1)
    kc = lane & (GRID_W - 1)
    second = lane >= GRID_W
    diff = kc - c + (NA_COLS - 1)
    cs = jnp.clip(c - NA_COLS // 2, 0, GRID_W - NA_COLS)
    valid = (kc >= cs) & (kc < cs + NA_COLS)
    tiles = []
    for dr in range(n_dr - 1):
        t = jnp.zeros((GRID_W, LANES), F32)
        for dc in range(n_dc):
            hit = diff == dc
            t = jnp.where(hit & ~second, rpb_ref[base + dr * n_dc + dc], t)
            t = jnp.where(hit & second, rpb_ref[base + (dr + 1) * n_dc + dc], t)
        tiles.append(jnp.where(valid, t, NEG_INF))
    for v in range(NA_ROWS):
        for m in range(NA_ROWS // 2):
            o_ref[0, 0, v, :, m * LANES:(m + 1) * LANES] = tiles[2 * m - v + NA_ROWS - 1]


def _bias_call(rpb):
    depth = rpb.shape[0]
    return pl.pallas_call(
        _bias_kernel,
        grid=(depth, N_HEADS_ATTN),
        in_specs=[pl.BlockSpec(memory_space=pltpu.SMEM)],
        out_specs=pl.BlockSpec((1, 1, NA_ROWS, GRID_W, NA_ROWS * GRID_W), lambda l, h: (l, h, 0, 0, 0)),
        out_shape=jax.ShapeDtypeStruct((depth, N_HEADS_ATTN, NA_ROWS, GRID_W, NA_ROWS * GRID_W), F32),
        compiler_params=_cparams(("arbitrary", "arbitrary")),
        name="na_bias",
    )(rpb.reshape(-1))


def _in_kernel(x_ref, m_ref, g_ref, w_ref, o_ref, *, n_ctx_tok, dec_seq):
    d = D_MODEL
    row = _mod_row(pl.program_id(0), x_ref.shape[0], n_ctx_tok, dec_seq)
    sh = m_ref[0, pl.ds(row, 1), 0:d]
    sc = m_ref[0, pl.ds(row, 1), d:2 * d]
    h = _rms(x_ref[...], g_ref[0]) * (1.0 + sc) + sh
    o_ref[...] = jnp.dot(h.astype(BF16), w_ref[0], preferred_element_type=F32)


def _in_call(x, mods, g1, w_in, l, n_ctx_tok, dec_seq):
    t, d = x.shape
    n = w_in.shape[-1]
    return pl.pallas_call(
        functools.partial(_in_kernel, n_ctx_tok=n_ctx_tok, dec_seq=dec_seq),
        grid=(t // TOK_BLK,),
        in_specs=[
            pl.BlockSpec((TOK_BLK, d), lambda i: (i, 0)),
            pl.BlockSpec((1, MOD_ROWS, N_MOD * d), lambda i: (l, 0, 0)),
            pl.BlockSpec((1, 1, d), lambda i: (l, 0, 0)),
            pl.BlockSpec((1, d, n), lambda i: (l, 0, 0)),
        ],
        out_specs=pl.BlockSpec((TOK_BLK, n), lambda i: (i, 0)),
        out_shape=jax.ShapeDtypeStruct((t, n), F32),
        compiler_params=_cparams(("arbitrary",)),
        name="in_proj",
    )(x, mods, g1, w_in)


def _mix_kernel(p_ref, cw_ref, bd_ref, ps_ref, o_ref, *, n_ctx_blk, seq_c, seq_l):
    blk = p_ref.shape[0]
    lseq = jnp.where(pl.program_id(0) < n_ctx_blk, seq_c, seq_l)
    rows = lax.broadcasted_iota(jnp.int32, (blk, A_WIDTH), 0)
    lane = lax.broadcasted_iota(jnp.int32, (blk, A_WIDTH), 1)
    tpos = rows & (lseq - 1)

    def shift_dn(x, m):
        return jnp.where(tpos >= m, pltpu.roll(x, m, 0), 0.0)

    def shift_up(x, m):
        return jnp.where(tpos + m < lseq, pltpu.roll(x, blk - m, 0), 0.0)

    a_in = p_ref[:, 0:A_WIDTH]
    a_b = p_ref[:, A_WIDTH:2 * A_WIDTH]
    a_c = p_ref[:, 2 * A_WIDTH:3 * A_WIDTH]
    u = a_c * a_in
    cw = cw_ref[0]
    y = cw[0:1] * shift_dn(u, 1) + cw[1:2] * u + cw[2:3] * shift_up(u, 1)
    o_ref[:, 0:A_WIDTH] = a_b * y

    p = p_ref[:, 3 * A_WIDTH:3 * A_WIDTH + B_WIDTH]
    f1 = p
    f2 = f1 + shift_up(f1, 1)
    f4 = f2 + shift_up(f2, 2)
    f8 = f4 + shift_up(f4, 4)
    b1 = shift_dn(p, 1)
    b2 = b1 + shift_dn(b1, 1)
    b4 = b2 + shift_dn(b2, 2)
    b8 = b4 + shift_dn(b4, 4)
    grp = jnp.right_shift(lane, POOL_GROUP_C.bit_length() - 1)
    wsum = jnp.where(grp == 0, b1 + f1, jnp.where(grp == 1, b2 + f2, jnp.where(grp == 2, b4 + f4, b8 + f8)))
    half = jnp.left_shift(1, grp)
    lo = jnp.maximum(tpos - half, 0)
    hi = jnp.minimum(tpos + half - 1, lseq - 1)
    cnt = (hi - lo + 1).astype(F32)
    dlt = wsum / cnt - p
    o_ref[:, A_WIDTH:A_WIDTH + B_WIDTH] = (
        jnp.dot(dlt.astype(BF16), bd_ref[0], preferred_element_type=F32) * ps_ref[0])


def _mix_call(proj, conv_w, pool_bd, pool_scale, l, n_ctx_tok, seq_c, seq_l):
    t = proj.shape[0]
    return pl.pallas_call(
        functools.partial(_mix_kernel, n_ctx_blk=n_ctx_tok // MIX_BLK, seq_c=seq_c, seq_l=seq_l),
        grid=(t // MIX_BLK,),
        in_specs=[
            pl.BlockSpec((MIX_BLK, 3 * A_WIDTH + B_WIDTH), lambda i: (i, 0)),
            pl.BlockSpec((1, 3, A_WIDTH), lambda i: (l, 0, 0)),
            pl.BlockSpec((1, B_WIDTH, B_WIDTH), lambda i: (l, 0, 0)),
            pl.BlockSpec((1, 1, B_WIDTH), lambda i: (l, 0, 0)),
        ],
        out_specs=pl.BlockSpec((MIX_BLK, A_WIDTH + B_WIDTH), lambda i: (i, 0)),
        out_shape=jax.ShapeDtypeStruct((t, A_WIDTH + B_WIDTH), F32),
        compiler_params=_cparams(("arbitrary",)),
        name="conv_pool",
    )(proj, conv_w, pool_bd, pool_scale)


def _attn_ctx_kernel(q_ref, k_ref, v_ref, y_ref, ko_ref, vo_ref):
    q = q_ref[...]
    k = k_ref[...]
    v = v_ref[...]
    lane = lax.broadcasted_iota(jnp.int32, q.shape, 1)
    kb = k.astype(BF16)
    vb = v.astype(BF16)
    outs = []
    for h in range(2):
        qh = jnp.where((lane >= HEAD_DIM) == (h == 1), q, 0.0).astype(BF16)
        s = lax.dot_general(qh, kb, (((1,), (1,)), ((), ())), preferred_element_type=F32) * ATTN_SCALE
        e = jnp.exp(s - jnp.max(s, axis=-1, keepdims=True))
        pr = e / jnp.sum(e, axis=-1, keepdims=True)
        outs.append(jnp.dot(pr.astype(BF16), vb, preferred_element_type=F32))
        ko_ref[0, h] = k[:, h * HEAD_DIM:(h + 1) * HEAD_DIM]
        vo_ref[0, h] = v[:, h * HEAD_DIM:(h + 1) * HEAD_DIM]
    y_ref[...] = jnp.where(lane < HEAD_DIM, outs[0], outs[1])


def _attn_ctx_call(proj, n_batch, seq):
    col0 = (3 * A_WIDTH + B_WIDTH) // LANES
    ncol = C_WIDTH // LANES
    kv_shape = jax.ShapeDtypeStruct((n_batch, N_HEADS_ATTN, seq, HEAD_DIM), F32)
    kv_spec = pl.BlockSpec((1, 2, seq, HEAD_DIM), lambda b, hp: (b, hp, 0, 0))
    return pl.pallas_call(
        _attn_ctx_kernel,
        grid=(n_batch, ncol),
        in_specs=[
            pl.BlockSpec((seq, LANES), lambda b, hp: (b, col0 + hp)),
            pl.BlockSpec((seq, LANES), lambda b, hp: (b, col0 + ncol + hp)),
            pl.BlockSpec((seq, LANES), lambda b, hp: (b, col0 + 2 * ncol + hp)),
        ],
        out_specs=[pl.BlockSpec((seq, LANES), lambda b, hp: (b, hp)), kv_spec, kv_spec],
        out_shape=[jax.ShapeDtypeStruct((n_batch * seq, C_WIDTH), F32), kv_shape, kv_shape],
        compiler_params=_cparams(("arbitrary", "arbitrary")),
        name="attn_ctx",
    )(proj, proj, proj)


def _attn_lat_kernel(q_ref, k_ref, v_ref, ck_ref, cv_ref, bv_ref, y_ref):
    n = q_ref.shape[0]
    rows = n // GRID_W
    nrow = min(NA_ROWS, rows)
    nwin = nrow * GRID_W
    lane = lax.broadcasted_iota(jnp.int32, (GRID_W, LANES), 1)
    ck = ck_ref[0, 0].astype(BF16)
    cv = cv_ref[0, 0].astype(BF16)
    dn = (((1,), (1,)), ((), ()))

    def body(r, carry):
        rs = jnp.clip(r - nrow // 2, 0, rows - nrow)
        q = q_ref[pl.ds(pl.multiple_of(r * GRID_W, GRID_W), GRID_W), :]
        koff = pl.multiple_of(rs * GRID_W, GRID_W)
        kw = k_ref[pl.ds(koff, nwin), :].astype(BF16)
        vw = v_ref[pl.ds(koff, nwin), :].astype(BF16)
        outs = []
        for h in range(2):
            qh = jnp.where((lane >= HEAD_DIM) == (h == 1), q, 0.0).astype(BF16)
            s_w = lax.dot_general(qh, kw, dn, preferred_element_type=F32) * ATTN_SCALE + bv_ref[0, h, r - rs]
            s_c = lax.dot_general(qh, ck, dn, preferred_element_type=F32) * ATTN_SCALE
            mx = jnp.maximum(jnp.max(s_w, axis=-1, keepdims=True), jnp.max(s_c, axis=-1, keepdims=True))
            e_w = jnp.exp(s_w - mx)
            e_c = jnp.exp(s_c - mx)
            z = jnp.sum(e_w, axis=-1, keepdims=True) + jnp.sum(e_c, axis=-1, keepdims=True)
            o = (jnp.dot(e_w.astype(BF16), vw, preferred_element_type=F32)
                 + jnp.dot(e_c.astype(BF16), cv, preferred_element_type=F32))
            outs.append(o / z)
        y_ref[pl.ds(pl.multiple_of(r * GRID_W, GRID_W), GRID_W), :] = jnp.where(lane < HEAD_DIM, outs[0], outs[1])
        return carry

    lax.fori_loop(0, rows, body, 0, unroll=ATTN_UNROLL)


def _attn_lat_call(proj, ck, cv, bias, l, n_ctx_tok, n_batch, seq):
    col0 = (3 * A_WIDTH + B_WIDTH) // LANES
    ncol = C_WIDTH // LANES
    row0 = n_ctx_tok // seq
    past = ck.shape[2]
    return pl.pallas_call(
        _attn_lat_kernel,
        grid=(n_batch, ncol),
        in_specs=[
            pl.BlockSpec((seq, LANES), lambda b, hp: (row0 + b, col0 + hp)),
            pl.BlockSpec((seq, LANES), lambda b, hp: (row0 + b, col0 + ncol + hp)),
            pl.BlockSpec((seq, LANES), lambda b, hp: (row0 + b, col0 + 2 * ncol + hp)),
            pl.BlockSpec((1, 1, past, LANES), lambda b, hp: (b, l, 0, hp)),
            pl.BlockSpec((1, 1, past, LANES), lambda b, hp: (b, l, 0, hp)),
            pl.BlockSpec((1, 2, NA_ROWS, GRID_W, NA_ROWS * GRID_W), lambda b, hp: (l, hp, 0, 0, 0)),
        ],
        out_specs=pl.BlockSpec((seq, LANES), lambda b, hp: (b, hp)),
        out_shape=jax.ShapeDtypeStruct((n_batch * seq, C_WIDTH), F32),
        compiler_params=_cparams(("arbitrary", "arbitrary")),
        name="attn_lat",
    )(proj, proj, proj, ck, cv, bias)


def _out_kernel(ycp_ref, yat_ref, x_ref, m_ref, g_ref, w_ref, xo_ref, h_ref, *, n_ctx_tok, dec_seq):
    d = D_MODEL
    half = A_WIDTH + B_WIDTH
    row = _mod_row(pl.program_id(0), x_ref.shape[0], n_ctx_tok, dec_seq)
    y = (jnp.dot(ycp_ref[...].astype(BF16), w_ref[0, 0:half], preferred_element_type=F32)
         + jnp.dot(yat_ref[...].astype(BF16), w_ref[0, half:2 * half], preferred_element_type=F32))
    x = x_ref[...] + m_ref[0, pl.ds(row, 1), 2 * d:3 * d] * y
    xo_ref[...] = x
    sh = m_ref[0, pl.ds(row, 1), 3 * d:4 * d]
    sc = m_ref[0, pl.ds(row, 1), 4 * d:5 * d]
    h_ref[...] = (_rms(x, g_ref[0]) * (1.0 + sc) + sh).astype(BF16)


def _out_call(ycp, yat, x, mods, g2, w_out, l, n_ctx_tok, dec_seq):
    t, d = x.shape
    half = ycp.shape[1]
    return pl.pallas_call(
        functools.partial(_out_kernel, n_ctx_tok=n_ctx_tok, dec_seq=dec_seq),
        grid=(t // TOK_BLK,),
        in_specs=[
            pl.BlockSpec((TOK_BLK, half), lambda i: (i, 0)),
            pl.BlockSpec((TOK_BLK, half), lambda i: (i, 0)),
            pl.BlockSpec((TOK_BLK, d), lambda i: (i, 0)),
            pl.BlockSpec((1, MOD_ROWS, N_MOD * d), lambda i: (l, 0, 0)),
            pl.BlockSpec((1, 1, d), lambda i: (l, 0, 0)),
            pl.BlockSpec((1, d, d), lambda i: (l, 0, 0)),
        ],
        out_specs=[pl.BlockSpec((TOK_BLK, d), lambda i: (i, 0)), pl.BlockSpec((TOK_BLK, d), lambda i: (i, 0))],
        out_shape=[jax.ShapeDtypeStruct((t, d), F32), jax.ShapeDtypeStruct((t, d), BF16)],
        compiler_params=_cparams(("arbitrary",)),
        name="out_proj",
    )(ycp, yat, x, mods, g2, w_out)


def _oem_sort_pairs(n):
    pairs = []

    def merge(lo, m, r):
        step = r * 2
        if step < m:
            merge(lo, m, step)
            merge(lo + r, m, step)
            pairs.extend((i, i + r) for i in range(lo + r, lo + m - r, step))
        else:
            pairs.append((lo, lo + r))

    def sort(lo, m):
        if m > 1:
            sort(lo, m // 2)
            sort(lo + m // 2, m // 2)
            merge(lo, m, 1)

    sort(0, n)
    return pairs


_SORT16 = _oem_sort_pairs(PEER_TOPK)
_CAND = [(a, b) for a in range(PEER_TOPK) for b in range(PEER_TOPK) if (a + 1) * (b + 1) <= PEER_TOPK]


def _cmp_exchange(vals, idxs, i, j):
    first = vals[i] >= vals[j]
    vals[i], vals[j] = jnp.maximum(vals[i], vals[j]), jnp.minimum(vals[i], vals[j])
    idxs[i], idxs[j] = jnp.where(first, idxs[i], idxs[j]), jnp.where(first, idxs[j], idxs[i])


def _top16(load):
    k = PEER_TOPK
    best = None
    for g in range(PEER_KEYS // k):
        vals = [load(g * k + i) for i in range(k)]
        idxs = [jnp.full(vals[0].shape, float(g * k + i), F32) for i in range(k)]
        for i, j in _SORT16:
            _cmp_exchange(vals, idxs, i, j)
        if best is not None:
            bv, bi = best
            for i in range(k):
                first = bv[i] >= vals[k - 1 - i]
                bi[i] = jnp.where(first, bi[i], idxs[k - 1 - i])
                bv[i] = jnp.maximum(bv[i], vals[k - 1 - i])
            stride = k // 2
            while stride >= 1:
                for i in range(k):
                    if i & stride == 0:
                        _cmp_exchange(bv, bi, i, i + stride)
                stride //= 2
            vals, idxs = bv, bi
        best = (vals, idxs)
    return best


def _gate_kernel(h_ref, w2_ref, o_ref, sc_ref, i1_ref, i2_ref, g_ref, gs_ref):
    nk = PEER_KEYS
    k = PEER_TOPK
    sc_ref[...] = lax.dot_general(w2_ref[0], h_ref[...], (((1,), (1,)), ((), ())), preferred_element_type=F32)
    tops = [_top16(lambda key, p=p: sc_ref[pl.ds((p * nk + key) * PEER_HEADS, PEER_HEADS), :]) for p in range(2)]
    (v1, x1), (v2, x2) = tops
    cand = [v1[a] + v2[b] for a, b in _CAND]
    slot_v, slot_1, slot_2 = [], [], []
    for it in range(k):
        live = [n for n, (a, b) in enumerate(_CAND) if (a + 1) * (b + 1) <= it + 1]
        mx = functools.reduce(jnp.maximum, [cand[n] for n in live])
        found = jnp.zeros(mx.shape, jnp.bool_)
        s1 = jnp.zeros(mx.shape, F32)
        s2 = jnp.zeros(mx.shape, F32)
        for n in live:
            a, b = _CAND[n]
            eq = cand[n] == mx
            sel = eq & ~found
            found = found | eq
            s1 = jnp.where(sel, x1[a], s1)
            s2 = jnp.where(sel, x2[b], s2)
            cand[n] = jnp.where(sel, -jnp.inf, cand[n])
        slot_v.append(mx)
        slot_1.append(s1)
        slot_2.append(s2)
    e = [jnp.exp(v - slot_v[0]) for v in slot_v]
    z = functools.reduce(jnp.add, e)
    gates = [x / z for x in e]
    i1_ref[...] = jnp.concatenate(slot_1, axis=0).T
    i2_ref[...] = jnp.concatenate(slot_2, axis=0).T
    g_ref[...] = jnp.concatenate(gates, axis=0).T
    pk = 2 * SUBLANES
    key_iota = lax.broadcasted_iota(jnp.int32, (nk // pk, pk, nk), 0) * pk + lax.broadcasted_iota(
        jnp.int32, (nk // pk, pk, nk), 1)
    key_iota = key_iota.astype(F32).astype(BF16)
    one = jnp.ones((), BF16)
    zero = jnp.zeros((), BF16)

    def row(ref, t):
        return jnp.broadcast_to(ref[pl.ds(t, 1), :], (pk, nk)).astype(BF16)[None]

    def per_token(t, carry):
        lt = jnp.where(key_iota == row(i1_ref, t), one, zero).reshape(nk, nk)
        rt = jnp.where(key_iota == row(i2_ref, t), row(g_ref, t), zero).reshape(nk, nk)
        gt = lax.dot_general(lt, rt, (((1,), (1,)), ((), ())), preferred_element_type=F32)
        gs_ref[pl.ds(pl.multiple_of(t * GATE_PITCH, SUBLANES), nk), :] = gt
        return carry

    lax.fori_loop(0, GATE_BLK, per_token, 0, unroll=GATE_UNROLL)
    for i1 in range(nk):
        o_ref[i1] = gs_ref[pl.ds(i1, GATE_BLK, stride=GATE_PITCH), :].astype(BF16)


def _gate_call(h2, w2t, l):
    t, d = h2.shape
    nrow = w2t.shape[1]
    return pl.pallas_call(
        _gate_kernel,
        grid=(t // GATE_BLK,),
        in_specs=[
            pl.BlockSpec((GATE_BLK, d), lambda i: (i, 0)),
            pl.BlockSpec((1, nrow, d), lambda i: (l, 0, 0)),
        ],
        out_specs=pl.BlockSpec((PEER_KEYS, GATE_BLK, PEER_KEYS), lambda i: (0, i, 0)),
        out_shape=jax.ShapeDtypeStruct((PEER_KEYS, t, PEER_KEYS), BF16),
        scratch_shapes=[
            pltpu.VMEM((nrow, GATE_BLK), F32),
            pltpu.VMEM((GATE_BLK, PEER_HEADS * PEER_TOPK), F32),
            pltpu.VMEM((GATE_BLK, PEER_HEADS * PEER_TOPK), F32),
            pltpu.VMEM((GATE_BLK, PEER_HEADS * PEER_TOPK), F32),
            pltpu.VMEM((GATE_BLK * GATE_PITCH, PEER_KEYS), F32),
        ],
        compiler_params=_cparams(("arbitrary",)),
        name="peer_gates",
    )(h2, w2t)


def _gelu(x):
    return 0.5 * x * (1.0 + jnp.tanh(0.7978845608028654 * (x + 0.044715 * (x * x * x))))


def _expert_kernel(h_ref, u_ref, v_ref, g_ref, x_ref, m_ref, o_ref, acc_ref, *, n_ctx_tok, dec_seq):
    d = D_MODEL
    j = pl.program_id(1)

    @pl.when(j == 0)
    def _():
        acc_ref[...] = jnp.zeros_like(acc_ref)

    a = lax.dot_general(h_ref[...], u_ref[0], (((1,), (1,)), ((), ())), preferred_element_type=F32)
    hid = jnp.concatenate(
        [(_gelu(a[:, q * PEER_KEYS:(q + 1) * PEER_KEYS]) * g_ref[q].astype(F32)).astype(BF16)
         for q in range(EXP_PLANES)], axis=1)
    acc_ref[...] += jnp.dot(hid, v_ref[0], preferred_element_type=F32)

    @pl.when(j == pl.num_programs(1) - 1)
    def _():
        row = _mod_row(pl.program_id(0), x_ref.shape[0], n_ctx_tok, dec_seq)
        o_ref[...] = x_ref[...] + m_ref[0, pl.ds(row, 1), 5 * d:6 * d] * acc_ref[...]


def _expert_call(h2, u, v, gates, x, mods, l, n_ctx_tok, dec_seq):
    t, d = x.shape
    ne = u.shape[1]
    return pl.pallas_call(
        functools.partial(_expert_kernel, n_ctx_tok=n_ctx_tok, dec_seq=dec_seq),
        grid=(t // EXP_TOK, ne // EXP_BLK),
        in_specs=[
            pl.BlockSpec((EXP_TOK, d), lambda i, j: (i, 0)),
            pl.BlockSpec((1, EXP_BLK, d), lambda i, j: (l, j, 0)),
            pl.BlockSpec((1, EXP_BLK, d), lambda i, j: (l, j, 0)),
            pl.BlockSpec((EXP_PLANES, EXP_TOK, PEER_KEYS), lambda i, j: (j, i, 0)),
            pl.BlockSpec((EXP_TOK, d), lambda i, j: (i, 0)),
            pl.BlockSpec((1, MOD_ROWS, N_MOD * d), lambda i, j: (l, 0, 0)),
        ],
        out_specs=pl.BlockSpec((EXP_TOK, d), lambda i, j: (i, 0)),
        out_shape=jax.ShapeDtypeStruct((t, d), F32),
        scratch_shapes=[pltpu.VMEM((EXP_TOK, d), F32)],
        compiler_params=_cparams(("arbitrary", "arbitrary")),
        name="peer_experts",
    )(h2, u, v, gates, x, mods)


def _final_kernel(x_ref, g_ref, o_ref):
    o_ref[...] = _rms(x_ref[...], g_ref[...])


def _final_call(x, g):
    t, d = x.shape
    return pl.pallas_call(
        _final_kernel,
        grid=(t // TOK_BLK,),
        in_specs=[pl.BlockSpec((TOK_BLK, d), lambda i: (i, 0)), pl.BlockSpec((1, d), lambda i: (0, 0))],
        out_specs=pl.BlockSpec((TOK_BLK, d), lambda i: (i, 0)),
        out_shape=jax.ShapeDtypeStruct((t, d), F32),
        compiler_params=_cparams(("arbitrary",)),
        name="final_norm",
    )(x, g.reshape(1, d))


def kernel(x_prompt, x_sample, c, cache_k, cache_v, c_ctx, norm1_g, norm2_g, w_ada, b_ada, w_in, conv_w,
           pool_w, pool_scale, rpb, w_out, peer_wq, peer_subkeys, peer_u, peer_v, final_g):
    n_b, seq_c, d = x_prompt.shape
    n_db, seq_l, _ = x_sample.shape
    depth = w_ada.shape[0]
    n_ctx_tok = n_b * seq_c
    assert d == D_MODEL and seq_l % GRID_W == 0 and seq_l // GRID_W >= NA_ROWS
    assert n_ctx_tok % MIX_BLK == 0 and seq_l % MIX_BLK == 0 and MIX_BLK % seq_c == 0
    assert 1 + n_db <= MOD_ROWS and (n_ctx_tok + n_db * seq_l) % EXP_TOK == 0

    x = jnp.concatenate([x_prompt.reshape(n_ctx_tok, d), x_sample.reshape(n_db * seq_l, d)], axis=0)
    cvec = jnp.concatenate([c_ctx[None, :], c, jnp.zeros((MOD_ROWS - 1 - n_db, d), F32)], axis=0)
    mods = _mods_call(cvec, w_ada, b_ada)
    w2t = _w2_call(peer_wq, peer_subkeys)
    w2t = w2t.transpose(0, 1, 3, 2, 4).reshape(depth, 2 * PEER_KEYS * PEER_HEADS, d)
    bias = _bias_call(rpb)
    past = cache_k.shape[3]
    ck = cache_k.transpose(0, 1, 3, 2, 4).reshape(n_db, depth, past, C_WIDTH)
    cv = cache_v.transpose(0, 1, 3, 2, 4).reshape(n_db, depth, past, C_WIDTH)
    w_in_b = w_in.astype(BF16)
    w_out_b = w_out.astype(BF16)
    u_b = peer_u.astype(BF16)
    v_b = peer_v.astype(BF16)
    eye = jnp.eye(B_WIDTH // POOL_GROUP_C, dtype=F32)
    pool_bd = jnp.einsum('gh,lgcd->lgchd', eye, pool_w).reshape(depth, B_WIDTH, B_WIDTH).astype(BF16)
    g1 = norm1_g.reshape(depth, 1, d)
    g2 = norm2_g.reshape(depth, 1, d)
    ps = pool_scale.reshape(depth, 1, B_WIDTH)

    ks_new, vs_new = [], []
    for l in range(depth):
        proj = _in_call(x, mods, g1, w_in_b, l, n_ctx_tok, seq_l)
        ycp = _mix_call(proj, conv_w, pool_bd, ps, l, n_ctx_tok, seq_c, seq_l)
        y_c, k_l, v_l = _attn_ctx_call(proj, n_b, seq_c)
        y_l = _attn_lat_call(proj, ck, cv, bias, l, n_ctx_tok, n_db, seq_l)
        ks_new.append(k_l)
        vs_new.append(v_l)
        yat = jnp.concatenate([y_c, y_l], axis=0)
        x, h2 = _out_call(ycp, yat, x, mods, g2, w_out_b, l, n_ctx_tok, seq_l)
        gates = _gate_call(h2, w2t, l)
        x = _expert_call(h2, u_b, v_b, gates, x, mods, l, n_ctx_tok, seq_l)
    y = _final_call(x, final_g)
    y_prompt = y[:n_ctx_tok].reshape(n_b, seq_c, d)
    y_sample = y[n_ctx_tok:].reshape(n_db, seq_l, d)
    return (y_prompt, y_sample, jnp.stack(ks_new, axis=1), jnp.stack(vs_new, axis=1))
```

```python
import functools

import numpy as np
import jax
import jax.numpy as jnp
from jax import lax
from jax.experimental import pallas as pl
from jax.experimental.pallas import tpu as pltpu

F32 = jnp.float32
BF16 = jnp.bfloat16

D_MODEL = 1024
DEPTH = 4
GRID_W = 64
A_WIDTH = D_MODEL // 4
B_WIDTH = D_MODEL // 4
C_WIDTH = D_MODEL // 2
N_HEADS_ATTN = 8
HEAD_DIM = C_WIDTH // N_HEADS_ATTN
POOL_GROUP_C = B_WIDTH // 4
NA_ROWS = 8
NA_COLS = 16
PEER_HEADS = 8
PEER_KEYS = 128
PEER_EXPERTS = PEER_KEYS * PEER_KEYS
PEER_TOPK = 16
PEER_HALF = 128
IN_COLS = 3 * A_WIDTH + B_WIDTH + 3 * C_WIDTH
N_MOD = 6
EPS = 1e-6
NEG_INF = -1e30
ATTN_SCALE = HEAD_DIM ** -0.5

LANES = 128
SUBLANES = 8
MOD_ROWS = 8
VMEM_LIMIT = 56 * 1024 * 1024

TOK_BLK = 512
MIX_BLK = 2048
GATE_BLK = 128
GATE_PITCH = PEER_KEYS + SUBLANES
GATE_UNROLL = 64
ATTN_UNROLL = 4
EXP_TOK = 1024
EXP_BLK = 1024
EXP_PLANES = EXP_BLK // PEER_KEYS


def _cparams(sem):
    return pltpu.CompilerParams(dimension_semantics=sem, vmem_limit_bytes=VMEM_LIMIT)


def _rms(x, g):
    return x * lax.rsqrt(jnp.mean(x * x, axis=-1, keepdims=True) + EPS) * g


def _mod_row(i, blk, n_ctx_tok, dec_seq):
    start = i * blk
    return jnp.where(start < n_ctx_tok, 0, 1 + jnp.maximum(start - n_ctx_tok, 0) // dec_seq)


def _mods_kernel(c_ref, w_ref, b_ref, o_ref):
    c = c_ref[...]
    s = (c * jax.nn.sigmoid(c)).astype(BF16)
    o_ref[0] = jnp.dot(s, w_ref[0].astype(BF16), preferred_element_type=F32) + b_ref[0]


def _mods_call(cvec, w_ada, b_ada):
    depth, d, n = w_ada.shape
    tn = 1536
    return pl.pallas_call(
        _mods_kernel,
        grid=(depth, n // tn),
        in_specs=[
            pl.BlockSpec((MOD_ROWS, d), lambda l, j: (0, 0)),
            pl.BlockSpec((1, d, tn), lambda l, j: (l, 0, j)),
            pl.BlockSpec((1, 1, tn), lambda l, j: (l, 0, j)),
        ],
        out_specs=pl.BlockSpec((1, MOD_ROWS, tn), lambda l, j: (l, 0, j)),
        out_shape=jax.ShapeDtypeStruct((depth, MOD_ROWS, n), F32),
        compiler_params=_cparams(("arbitrary", "arbitrary")),
        name="mods",
    )(cvec, w_ada, b_ada.reshape(depth, 1, n))


def _w2_kernel(k_ref, wq_ref, o_ref):
    o_ref[0, 0, 0] = lax.dot_general(
        k_ref[0, 0, 0], wq_ref[0], (((1,), (1,)), ((), ())),
        precision=lax.Precision.HIGHEST, preferred_element_type=F32).astype(BF16)


def _w2_call(peer_wq, peer_subkeys):
    depth, d, _ = peer_wq.shape
    return pl.pallas_call(
        _w2_kernel,
        grid=(depth, 2, PEER_HEADS),
        in_specs=[
            pl.BlockSpec((1, 1, 1, PEER_KEYS, PEER_HALF), lambda l, p, h: (l, h, p, 0, 0)),
            pl.BlockSpec((1, d, PEER_HALF), lambda l, p, h: (l, 0, 2 * h + p)),
        ],
        out_specs=pl.BlockSpec((1, 1, 1, PEER_KEYS, d), lambda l, p, h: (l, p, h, 0, 0)),
        out_shape=jax.ShapeDtypeStruct((depth, 2, PEER_HEADS, PEER_KEYS, d), BF16),
        compiler_params=_cparams(("arbitrary",) * 3),
        name="peer_w2",
    )(peer_subkeys, peer_wq)


def _bias_kernel(rpb_ref, o_ref):
    l = pl.program_id(0)
    h = pl.program_id(1)
    n_dr = 2 * NA_ROWS - 1
    n_dc = 2 * NA_COLS - 1
    base = (l * N_HEADS_ATTN + h) * (n_dr * n_dc)
    c = lax.broadcasted_iota(jnp.int32, (GRID_W, LANES), 0)
    lane = lax.broadcasted_iota(jnp.int32, (GRID_W, LANES), 1)
    kc = lane & (GRID_W - 1)
    second = lane >= GRID_W
    diff = kc - c + (NA_COLS - 1)
    cs = jnp.clip(c - NA_COLS // 2, 0, GRID_W - NA_COLS)
    valid = (kc >= cs) & (kc < cs + NA_COLS)
    tiles = []
    for dr in range(n_dr - 1):
        t = jnp.zeros((GRID_W, LANES), F32)
        for dc in range(n_dc):
            hit = diff == dc
            t = jnp.where(hit & ~second, rpb_ref[base + dr * n_dc + dc], t)
            t = jnp.where(hit & second, rpb_ref[base + (dr + 1) * n_dc + dc], t)
        tiles.append(jnp.where(valid, t, NEG_INF))
    for v in range(NA_ROWS):
        for m in range(NA_ROWS // 2):
            o_ref[0, 0, v, :, m * LANES:(m + 1) * LANES] = tiles[2 * m - v + NA_ROWS - 1]


def _bias_call(rpb):
    depth = rpb.shape[0]
    return pl.pallas_call(
        _bias_kernel,
        grid=(depth, N_HEADS_ATTN),
        in_specs=[pl.BlockSpec(memory_space=pltpu.SMEM)],
        out_specs=pl.BlockSpec((1, 1, NA_ROWS, GRID_W, NA_ROWS * GRID_W), lambda l, h: (l, h, 0, 0, 0)),
        out_shape=jax.ShapeDtypeStruct((depth, N_HEADS_ATTN, NA_ROWS, GRID_W, NA_ROWS * GRID_W), F32),
        compiler_params=_cparams(("arbitrary", "arbitrary")),
        name="na_bias",
    )(rpb.reshape(-1))


def _in_kernel(x_ref, m_ref, g_ref, w_ref, o_ref, *, n_ctx_tok, dec_seq):
    d = D_MODEL
    row = _mod_row(pl.program_id(0), x_ref.shape[0], n_ctx_tok, dec_seq)
    sh = m_ref[0, pl.ds(row, 1), 0:d]
    sc = m_ref[0, pl.ds(row, 1), d:2 * d]
    h = _rms(x_ref[...], g_ref[0]) * (1.0 + sc) + sh
    o_ref[...] = jnp.dot(h.astype(BF16), w_ref[0], preferred_element_type=F32)


def _in_call(x, mods, g1, w_in, l, n_ctx_tok, dec_seq):
    t, d = x.shape
    n = w_in.shape[-1]
    return pl.pallas_call(
        functools.partial(_in_kernel, n_ctx_tok=n_ctx_tok, dec_seq=dec_seq),
        grid=(t // TOK_BLK,),
        in_specs=[
            pl.BlockSpec((TOK_BLK, d), lambda i: (i, 0)),
            pl.BlockSpec((1, MOD_ROWS, N_MOD * d), lambda i: (l, 0, 0)),
            pl.BlockSpec((1, 1, d), lambda i: (l, 0, 0)),
            pl.BlockSpec((1, d, n), lambda i: (l, 0, 0)),
        ],
        out_specs=pl.BlockSpec((TOK_BLK, n), lambda i: (i, 0)),
        out_shape=jax.ShapeDtypeStruct((t, n), F32),
        compiler_params=_cparams(("arbitrary",)),
        name="in_proj",
    )(x, mods, g1, w_in)


def _mix_kernel(p_ref, cw_ref, bd_ref, ps_ref, o_ref, *, n_ctx_blk, seq_c, seq_l):
    blk = p_ref.shape[0]
    lseq = jnp.where(pl.program_id(0) < n_ctx_blk, seq_c, seq_l)
    rows = lax.broadcasted_iota(jnp.int32, (blk, A_WIDTH), 0)
    lane = lax.broadcasted_iota(jnp.int32, (blk, A_WIDTH), 1)
    tpos = rows & (lseq - 1)

    def shift_dn(x, m):
        return jnp.where(tpos >= m, pltpu.roll(x, m, 0), 0.0)

    def shift_up(x, m):
        return jnp.where(tpos + m < lseq, pltpu.roll(x, blk - m, 0), 0.0)

    a_in = p_ref[:, 0:A_WIDTH]
    a_b = p_ref[:, A_WIDTH:2 * A_WIDTH]
    a_c = p_ref[:, 2 * A_WIDTH:3 * A_WIDTH]
    u = a_c * a_in
    cw = cw_ref[0]
    y = cw[0:1] * shift_dn(u, 1) + cw[1:2] * u + cw[2:3] * shift_up(u, 1)
    o_ref[:, 0:A_WIDTH] = a_b * y

    p = p_ref[:, 3 * A_WIDTH:3 * A_WIDTH + B_WIDTH]
    f1 = p
    f2 = f1 + shift_up(f1, 1)
    f4 = f2 + shift_up(f2, 2)
    f8 = f4 + shift_up(f4, 4)
    b1 = shift_dn(p, 1)
    b2 = b1 + shift_dn(b1, 1)
    b4 = b2 + shift_dn(b2, 2)
    b8 = b4 + shift_dn(b4, 4)
    grp = jnp.right_shift(lane, POOL_GROUP_C.bit_length() - 1)
    wsum = jnp.where(grp == 0, b1 + f1, jnp.where(grp == 1, b2 + f2, jnp.where(grp == 2, b4 + f4, b8 + f8)))
    half = jnp.left_shift(1, grp)
    lo = jnp.maximum(tpos - half, 0)
    hi = jnp.minimum(tpos + half - 1, lseq - 1)
    cnt = (hi - lo + 1).astype(F32)
    dlt = wsum / cnt - p
    o_ref[:, A_WIDTH:A_WIDTH + B_WIDTH] = (
        jnp.dot(dlt.astype(BF16), bd_ref[0], preferred_element_type=F32) * ps_ref[0])


def _mix_call(proj, conv_w, pool_bd, pool_scale, l, n_ctx_tok, seq_c, seq_l):
    t = proj.shape[0]
    return pl.pallas_call(
        functools.partial(_mix_kernel, n_ctx_blk=n_ctx_tok // MIX_BLK, seq_c=seq_c, seq_l=seq_l),
        grid=(t // MIX_BLK,),
        in_specs=[
            pl.BlockSpec((MIX_BLK, 3 * A_WIDTH + B_WIDTH), lambda i: (i, 0)),
            pl.BlockSpec((1, 3, A_WIDTH), lambda i: (l, 0, 0)),
            pl.BlockSpec((1, B_WIDTH, B_WIDTH), lambda i: (l, 0, 0)),
            pl.BlockSpec((1, 1, B_WIDTH), lambda i: (l, 0, 0)),
        ],
        out_specs=pl.BlockSpec((MIX_BLK, A_WIDTH + B_WIDTH), lambda i: (i, 0)),
        out_shape=jax.ShapeDtypeStruct((t, A_WIDTH + B_WIDTH), F32),
        compiler_params=_cparams(("arbitrary",)),
        name="conv_pool",
    )(proj, conv_w, pool_bd, pool_scale)


def _attn_ctx_kernel(q_ref, k_ref, v_ref, y_ref, ko_ref, vo_ref):
    q = q_ref[...]
    k = k_ref[...]
    v = v_ref[...]
    lane = lax.broadcasted_iota(jnp.int32, q.shape, 1)
    kb = k.astype(BF16)
    vb = v.astype(BF16)
    outs = []
    for h in range(2):
        qh = jnp.where((lane >= HEAD_DIM) == (h == 1), q, 0.0).astype(BF16)
        s = lax.dot_general(qh, kb, (((1,), (1,)), ((), ())), preferred_element_type=F32) * ATTN_SCALE
        e = jnp.exp(s - jnp.max(s, axis=-1, keepdims=True))
        pr = e / jnp.sum(e, axis=-1, keepdims=True)
        outs.append(jnp.dot(pr.astype(BF16), vb, preferred_element_type=F32))
        ko_ref[0, h] = k[:, h * HEAD_DIM:(h + 1) * HEAD_DIM]
        vo_ref[0, h] = v[:, h * HEAD_DIM:(h + 1) * HEAD_DIM]
    y_ref[...] = jnp.where(lane < HEAD_DIM, outs[0], outs[1])


def _attn_ctx_call(proj, n_batch, seq):
    col0 = (3 * A_WIDTH + B_WIDTH) // LANES
    ncol = C_WIDTH // LANES
    kv_shape = jax.ShapeDtypeStruct((n_batch, N_HEADS_ATTN, seq, HEAD_DIM), F32)
    kv_spec = pl.BlockSpec((1, 2, seq, HEAD_DIM), lambda b, hp: (b, hp, 0, 0))
    return pl.pallas_call(
        _attn_ctx_kernel,
        grid=(n_batch, ncol),
        in_specs=[
            pl.BlockSpec((seq, LANES), lambda b, hp: (b, col0 + hp)),
            pl.BlockSpec((seq, LANES), lambda b, hp: (b, col0 + ncol + hp)),
            pl.BlockSpec((seq, LANES), lambda b, hp: (b, col0 + 2 * ncol + hp)),
        ],
        out_specs=[pl.BlockSpec((seq, LANES), lambda b, hp: (b, hp)), kv_spec, kv_spec],
        out_shape=[jax.ShapeDtypeStruct((n_batch * seq, C_WIDTH), F32), kv_shape, kv_shape],
        compiler_params=_cparams(("arbitrary", "arbitrary")),
        name="attn_ctx",
    )(proj, proj, proj)


def _attn_lat_kernel(q_ref, k_ref, v_ref, ck_ref, cv_ref, bv_ref, y_ref, s_ref, e_ref, oc_ref, zi_ref):
    n = q_ref.shape[0]
    rows = n // GRID_W
    nrow = min(NA_ROWS, rows)
    nwin = nrow * GRID_W
    dn = (((1,), (1,)), ((), ()))
    ck = ck_ref[0, 0].astype(BF16)
    cv = cv_ref[0, 0].astype(BF16)
    lane_all = lax.broadcasted_iota(jnp.int32, (n, LANES), 1)
    lane = lax.broadcasted_iota(jnp.int32, (GRID_W, LANES), 1)

    for h in range(2):
        second = h == 1

        def scores(r, carry):
            rs = jnp.clip(r - nrow // 2, 0, rows - nrow)
            qrow = pl.ds(pl.multiple_of(r * GRID_W, GRID_W), GRID_W)
            qh = jnp.where((lane >= HEAD_DIM) == second, q_ref[qrow, :], 0.0).astype(BF16)
            kw = k_ref[pl.ds(pl.multiple_of(rs * GRID_W, GRID_W), nwin), :].astype(BF16)
            s_ref[qrow, :] = (lax.dot_general(qh, kw, dn, preferred_element_type=F32) * ATTN_SCALE
                              + bv_ref[0, h, r - rs])
            return carry

        lax.fori_loop(0, rows, scores, 0, unroll=ATTN_UNROLL)

        qh_all = jnp.where((lane_all >= HEAD_DIM) == second, q_ref[...], 0.0).astype(BF16)
        s_c = lax.dot_general(qh_all, ck, dn, preferred_element_type=F32) * ATTN_SCALE
        s_w = s_ref[...]
        mx = jnp.maximum(jnp.max(s_w, axis=-1, keepdims=True), jnp.max(s_c, axis=-1, keepdims=True))
        e_w = jnp.exp(s_w - mx)
        e_c = jnp.exp(s_c - mx)
        z = jnp.sum(e_w, axis=-1, keepdims=True) + jnp.sum(e_c, axis=-1, keepdims=True)
        e_ref[...] = e_w.astype(BF16)
        oc_ref[...] = jnp.dot(e_c.astype(BF16), cv, preferred_element_type=F32)
        zi_ref[...] = jnp.broadcast_to(1.0 / z, (n, LANES))

        def values(r, carry):
            rs = jnp.clip(r - nrow // 2, 0, rows - nrow)
            qrow = pl.ds(pl.multiple_of(r * GRID_W, GRID_W), GRID_W)
            vw = v_ref[pl.ds(pl.multiple_of(rs * GRID_W, GRID_W), nwin), :].astype(BF16)
            o = (jnp.dot(e_ref[qrow, :], vw, preferred_element_type=F32) + oc_ref[qrow, :]) * zi_ref[qrow, :]
            if second:
                o = jnp.where(lane < HEAD_DIM, y_ref[qrow, :], o)
            y_ref[qrow, :] = o
            return carry

        lax.fori_loop(0, rows, values, 0, unroll=ATTN_UNROLL)


def _attn_lat_call(proj, ck, cv, bias, l, n_ctx_tok, n_batch, seq):
    col0 = (3 * A_WIDTH + B_WIDTH) // LANES
    ncol = C_WIDTH // LANES
    row0 = n_ctx_tok // seq
    past = ck.shape[2]
    return pl.pallas_call(
        _attn_lat_kernel,
        grid=(n_batch, ncol),
        in_specs=[
            pl.BlockSpec((seq, LANES), lambda b, hp: (row0 + b, col0 + hp)),
            pl.BlockSpec((seq, LANES), lambda b, hp: (row0 + b, col0 + ncol + hp)),
            pl.BlockSpec((seq, LANES), lambda b, hp: (row0 + b, col0 + 2 * ncol + hp)),
            pl.BlockSpec((1, 1, past, LANES), lambda b, hp: (b, l, 0, hp)),
            pl.BlockSpec((1, 1, past, LANES), lambda b, hp: (b, l, 0, hp)),
            pl.BlockSpec((1, 2, NA_ROWS, GRID_W, NA_ROWS * GRID_W), lambda b, hp: (l, hp, 0, 0, 0)),
        ],
        out_specs=pl.BlockSpec((seq, LANES), lambda b, hp: (b, hp)),
        out_shape=jax.ShapeDtypeStruct((n_batch * seq, C_WIDTH), F32),
        scratch_shapes=[pltpu.VMEM((seq, NA_ROWS * GRID_W), F32), pltpu.VMEM((seq, NA_ROWS * GRID_W), BF16),
                        pltpu.VMEM((seq, LANES), F32), pltpu.VMEM((seq, LANES), F32)],
        compiler_params=_cparams(("arbitrary", "arbitrary")),
        name="attn_lat",
    )(proj, proj, proj, ck, cv, bias)


def _out_kernel(ycp_ref, yat_ref, x_ref, m_ref, g_ref, w_ref, xo_ref, h_ref, *, n_ctx_tok, dec_seq):
    d = D_MODEL
    half = A_WIDTH + B_WIDTH
    row = _mod_row(pl.program_id(0), x_ref.shape[0], n_ctx_tok, dec_seq)
    y = (jnp.dot(ycp_ref[...].astype(BF16), w_ref[0, 0:half], preferred_element_type=F32)
         + jnp.dot(yat_ref[...].astype(BF16), w_ref[0, half:2 * half], preferred_element_type=F32))
    x = x_ref[...] + m_ref[0, pl.ds(row, 1), 2 * d:3 * d] * y
    xo_ref[...] = x
    sh = m_ref[0, pl.ds(row, 1), 3 * d:4 * d]
    sc = m_ref[0, pl.ds(row, 1), 4 * d:5 * d]
    h_ref[...] = (_rms(x, g_ref[0]) * (1.0 + sc) + sh).astype(BF16)


def _out_call(ycp, yat, x, mods, g2, w_out, l, n_ctx_tok, dec_seq):
    t, d = x.shape
    half = ycp.shape[1]
    return pl.pallas_call(
        functools.partial(_out_kernel, n_ctx_tok=n_ctx_tok, dec_seq=dec_seq),
        grid=(t // TOK_BLK,),
        in_specs=[
            pl.BlockSpec((TOK_BLK, half), lambda i: (i, 0)),
            pl.BlockSpec((TOK_BLK, half), lambda i: (i, 0)),
            pl.BlockSpec((TOK_BLK, d), lambda i: (i, 0)),
            pl.BlockSpec((1, MOD_ROWS, N_MOD * d), lambda i: (l, 0, 0)),
            pl.BlockSpec((1, 1, d), lambda i: (l, 0, 0)),
            pl.BlockSpec((1, d, d), lambda i: (l, 0, 0)),
        ],
        out_specs=[pl.BlockSpec((TOK_BLK, d), lambda i: (i, 0)), pl.BlockSpec((TOK_BLK, d), lambda i: (i, 0))],
        out_shape=[jax.ShapeDtypeStruct((t, d), F32), jax.ShapeDtypeStruct((t, d), BF16)],
        compiler_params=_cparams(("arbitrary",)),
        name="out_proj",
    )(ycp, yat, x, mods, g2, w_out)


def _oem_sort_pairs(n):
    pairs = []

    def merge(lo, m, r):
        step = r * 2
        if step < m:
            merge(lo, m, step)
            merge(lo + r, m, step)
            pairs.extend((i, i + r) for i in range(lo + r, lo + m - r, step))
        else:
            pairs.append((lo, lo + r))

    def sort(lo, m):
        if m > 1:
            sort(lo, m // 2)
            sort(lo + m // 2, m // 2)
            merge(lo, m, 1)

    sort(0, n)
    return pairs


_SORT16 = _oem_sort_pairs(PEER_TOPK)
_CAND = [(a, b) for a in range(PEER_TOPK) for b in range(PEER_TOPK) if (a + 1) * (b + 1) <= PEER_TOPK]


def _cmp_exchange(vals, idxs, i, j):
    first = vals[i] >= vals[j]
    vals[i], vals[j] = jnp.maximum(vals[i], vals[j]), jnp.minimum(vals[i], vals[j])
    idxs[i], idxs[j] = jnp.where(first, idxs[i], idxs[j]), jnp.where(first, idxs[j], idxs[i])


def _top16(load):
    k = PEER_TOPK
    best = None
    for g in range(PEER_KEYS // k):
        vals = [load(g * k + i) for i in range(k)]
        idxs = [jnp.full(vals[0].shape, float(g * k + i), F32) for i in range(k)]
        for i, j in _SORT16:
            _cmp_exchange(vals, idxs, i, j)
        if best is not None:
            bv, bi = best
            for i in range(k):
                first = bv[i] >= vals[k - 1 - i]
                bi[i] = jnp.where(first, bi[i], idxs[k - 1 - i])
                bv[i] = jnp.maximum(bv[i], vals[k - 1 - i])
            stride = k // 2
            while stride >= 1:
                for i in range(k):
                    if i & stride == 0:
                        _cmp_exchange(bv, bi, i, i + stride)
                stride //= 2
            vals, idxs = bv, bi
        best = (vals, idxs)
    return best


def _gate_kernel(h_ref, w2_ref, o_ref, sc_ref, i1_ref, i2_ref, g_ref, gs_ref):
    nk = PEER_KEYS
    k = PEER_TOPK
    sc_ref[...] = lax.dot_general(w2_ref[0], h_ref[...], (((1,), (1,)), ((), ())), preferred_element_type=F32)
    tops = [_top16(lambda key, p=p: sc_ref[pl.ds((p * nk + key) * PEER_HEADS, PEER_HEADS), :]) for p in range(2)]
    (v1, x1), (v2, x2) = tops
    cand = [v1[a] + v2[b] for a, b in _CAND]
    slot_v, slot_1, slot_2 = [], [], []
    for it in range(k):
        live = [n for n, (a, b) in enumerate(_CAND) if (a + 1) * (b + 1) <= it + 1]
        mx = functools.reduce(jnp.maximum, [cand[n] for n in live])
        found = jnp.zeros(mx.shape, jnp.bool_)
        s1 = jnp.zeros(mx.shape, F32)
        s2 = jnp.zeros(mx.shape, F32)
        for n in live:
            a, b = _CAND[n]
            eq = cand[n] == mx
            sel = eq & ~found
            found = found | eq
            s1 = jnp.where(sel, x1[a], s1)
            s2 = jnp.where(sel, x2[b], s2)
            cand[n] = jnp.where(sel, -jnp.inf, cand[n])
        slot_v.append(mx)
        slot_1.append(s1)
        slot_2.append(s2)
    e = [jnp.exp(v - slot_v[0]) for v in slot_v]
    z = functools.reduce(jnp.add, e)
    gates = [x / z for x in e]
    i1_ref[...] = jnp.concatenate(slot_1, axis=0).T
    i2_ref[...] = jnp.concatenate(slot_2, axis=0).T
    g_ref[...] = jnp.concatenate(gates, axis=0).T
    pk = 2 * SUBLANES
    key_iota = lax.broadcasted_iota(jnp.int32, (nk // pk, pk, nk), 0) * pk + lax.broadcasted_iota(
        jnp.int32, (nk // pk, pk, nk), 1)
    key_iota = key_iota.astype(F32).astype(BF16)
    one = jnp.ones((), BF16)
    zero = jnp.zeros((), BF16)

    def row(ref, t):
        return jnp.broadcast_to(ref[pl.ds(t, 1), :], (pk, nk)).astype(BF16)[None]

    def per_token(t, carry):
        lt = jnp.where(key_iota == row(i1_ref, t), one, zero).reshape(nk, nk)
        rt = jnp.where(key_iota == row(i2_ref, t), row(g_ref, t), zero).reshape(nk, nk)
        gt = lax.dot_general(lt, rt, (((1,), (1,)), ((), ())), preferred_element_type=F32)
        gs_ref[pl.ds(pl.multiple_of(t * GATE_PITCH, SUBLANES), nk), :] = gt
        return carry

    lax.fori_loop(0, GATE_BLK, per_token, 0, unroll=GATE_UNROLL)
    for i1 in range(nk):
        o_ref[i1] = gs_ref[pl.ds(i1, GATE_BLK, stride=GATE_PITCH), :].astype(BF16)


def _gate_call(h2, w2t, l):
    t, d = h2.shape
    nrow = w2t.shape[1]
    return pl.pallas_call(
        _gate_kernel,
        grid=(t // GATE_BLK,),
        in_specs=[
            pl.BlockSpec((GATE_BLK, d), lambda i: (i, 0)),
            pl.BlockSpec((1, nrow, d), lambda i: (l, 0, 0)),
        ],
        out_specs=pl.BlockSpec((PEER_KEYS, GATE_BLK, PEER_KEYS), lambda i: (0, i, 0)),
        out_shape=jax.ShapeDtypeStruct((PEER_KEYS, t, PEER_KEYS), BF16),
        scratch_shapes=[
            pltpu.VMEM((nrow, GATE_BLK), F32),
            pltpu.VMEM((GATE_BLK, PEER_HEADS * PEER_TOPK), F32),
            pltpu.VMEM((GATE_BLK, PEER_HEADS * PEER_TOPK), F32),
            pltpu.VMEM((GATE_BLK, PEER_HEADS * PEER_TOPK), F32),
            pltpu.VMEM((GATE_BLK * GATE_PITCH, PEER_KEYS), F32),
        ],
        compiler_params=_cparams(("arbitrary",)),
        name="peer_gates",
    )(h2, w2t)


def _gelu(x):
    return 0.5 * x * (1.0 + jnp.tanh(0.7978845608028654 * (x + 0.044715 * (x * x * x))))


def _expert_kernel(h_ref, u_ref, v_ref, g_ref, x_ref, m_ref, o_ref, acc_ref, *, n_ctx_tok, dec_seq):
    d = D_MODEL
    j = pl.program_id(1)

    @pl.when(j == 0)
    def _():
        acc_ref[...] = jnp.zeros_like(acc_ref)

    a = lax.dot_general(h_ref[...], u_ref[0].astype(BF16), (((1,), (1,)), ((), ())), preferred_element_type=F32)
    hid = jnp.concatenate(
        [(_gelu(a[:, q * PEER_KEYS:(q + 1) * PEER_KEYS]) * g_ref[q].astype(F32)).astype(BF16)
         for q in range(EXP_PLANES)], axis=1)
    acc_ref[...] += jnp.dot(hid, v_ref[0].astype(BF16), preferred_element_type=F32)

    @pl.when(j == pl.num_programs(1) - 1)
    def _():
        row = _mod_row(pl.program_id(0), x_ref.shape[0], n_ctx_tok, dec_seq)
        o_ref[...] = x_ref[...] + m_ref[0, pl.ds(row, 1), 5 * d:6 * d] * acc_ref[...]


def _expert_call(h2, u, v, gates, x, mods, l, n_ctx_tok, dec_seq):
    t, d = x.shape
    ne = u.shape[1]
    return pl.pallas_call(
        functools.partial(_expert_kernel, n_ctx_tok=n_ctx_tok, dec_seq=dec_seq),
        grid=(t // EXP_TOK, ne // EXP_BLK),
        in_specs=[
            pl.BlockSpec((EXP_TOK, d), lambda i, j: (i, 0)),
            pl.BlockSpec((1, EXP_BLK, d), lambda i, j: (l, j, 0)),
            pl.BlockSpec((1, EXP_BLK, d), lambda i, j: (l, j, 0)),
            pl.BlockSpec((EXP_PLANES, EXP_TOK, PEER_KEYS), lambda i, j: (j, i, 0)),
            pl.BlockSpec((EXP_TOK, d), lambda i, j: (i, 0)),
            pl.BlockSpec((1, MOD_ROWS, N_MOD * d), lambda i, j: (l, 0, 0)),
        ],
        out_specs=pl.BlockSpec((EXP_TOK, d), lambda i, j: (i, 0)),
        out_shape=jax.ShapeDtypeStruct((t, d), F32),
        scratch_shapes=[pltpu.VMEM((EXP_TOK, d), F32)],
        compiler_params=_cparams(("arbitrary", "arbitrary")),
        name="peer_experts",
    )(h2, u, v, gates, x, mods)


def _final_kernel(x_ref, g_ref, o_ref):
    o_ref[...] = _rms(x_ref[...], g_ref[...])


def _final_call(x, g):
    t, d = x.shape
    return pl.pallas_call(
        _final_kernel,
        grid=(t // TOK_BLK,),
        in_specs=[pl.BlockSpec((TOK_BLK, d), lambda i: (i, 0)), pl.BlockSpec((1, d), lambda i: (0, 0))],
        out_specs=pl.BlockSpec((TOK_BLK, d), lambda i: (i, 0)),
        out_shape=jax.ShapeDtypeStruct((t, d), F32),
        compiler_params=_cparams(("arbitrary",)),
        name="final_norm",
    )(x, g.reshape(1, d))


def kernel(x_prompt, x_sample, c, cache_k, cache_v, c_ctx, norm1_g, norm2_g, w_ada, b_ada, w_in, conv_w,
           pool_w, pool_scale, rpb, w_out, peer_wq, peer_subkeys, peer_u, peer_v, final_g):
    n_b, seq_c, d = x_prompt.shape
    n_db, seq_l, _ = x_sample.shape
    depth = w_ada.shape[0]
    n_ctx_tok = n_b * seq_c
    assert d == D_MODEL and seq_l % GRID_W == 0 and seq_l // GRID_W >= NA_ROWS
    assert n_ctx_tok % MIX_BLK == 0 and seq_l % MIX_BLK == 0 and MIX_BLK % seq_c == 0
    assert 1 + n_db <= MOD_ROWS and (n_ctx_tok + n_db * seq_l) % EXP_TOK == 0

    x = jnp.concatenate([x_prompt.reshape(n_ctx_tok, d), x_sample.reshape(n_db * seq_l, d)], axis=0)
    cvec = jnp.concatenate([c_ctx[None, :], c, jnp.zeros((MOD_ROWS - 1 - n_db, d), F32)], axis=0)
    mods = _mods_call(cvec, w_ada, b_ada)
    w2t = _w2_call(peer_wq, peer_subkeys)
    w2t = w2t.transpose(0, 1, 3, 2, 4).reshape(depth, 2 * PEER_KEYS * PEER_HEADS, d)
    bias = _bias_call(rpb)
    past = cache_k.shape[3]
    ck = cache_k.transpose(0, 1, 3, 2, 4).reshape(n_db, depth, past, C_WIDTH)
    cv = cache_v.transpose(0, 1, 3, 2, 4).reshape(n_db, depth, past, C_WIDTH)
    w_in_b = w_in.astype(BF16)
    w_out_b = w_out.astype(BF16)
    eye = jnp.eye(B_WIDTH // POOL_GROUP_C, dtype=F32)
    pool_bd = jnp.einsum('gh,lgcd->lgchd', eye, pool_w).reshape(depth, B_WIDTH, B_WIDTH).astype(BF16)
    g1 = norm1_g.reshape(depth, 1, d)
    g2 = norm2_g.reshape(depth, 1, d)
    ps = pool_scale.reshape(depth, 1, B_WIDTH)

    ks_new, vs_new = [], []
    for l in range(depth):
        proj = _in_call(x, mods, g1, w_in_b, l, n_ctx_tok, seq_l)
        ycp = _mix_call(proj, conv_w, pool_bd, ps, l, n_ctx_tok, seq_c, seq_l)
        y_c, k_l, v_l = _attn_ctx_call(proj, n_b, seq_c)
        y_l = _attn_lat_call(proj, ck, cv, bias, l, n_ctx_tok, n_db, seq_l)
        ks_new.append(k_l)
        vs_new.append(v_l)
        yat = jnp.concatenate([y_c, y_l], axis=0)
        x, h2 = _out_call(ycp, yat, x, mods, g2, w_out_b, l, n_ctx_tok, seq_l)
        gates = _gate_call(h2, w2t, l)
        x = _expert_call(h2, peer_u, peer_v, gates, x, mods, l, n_ctx_tok, seq_l)
    y = _final_call(x, final_g)
    y_prompt = y[:n_ctx_tok].reshape(n_b, seq_c, d)
    y_sample = y[n_ctx_tok:].reshape(n_db, seq_l, d)
    return (y_prompt, y_sample, jnp.stack(ks_new, axis=1), jnp.stack(vs_new, axis=1))
```

```python
import functools

import numpy as np
import jax
import jax.numpy as jnp
from jax import lax
from jax.experimental import pallas as pl
from jax.experimental.pallas import tpu as pltpu

F32 = jnp.float32
BF16 = jnp.bfloat16

D_MODEL = 1024
DEPTH = 4
GRID_W = 64
A_WIDTH = D_MODEL // 4
B_WIDTH = D_MODEL // 4
C_WIDTH = D_MODEL // 2
N_HEADS_ATTN = 8
HEAD_DIM = C_WIDTH // N_HEADS_ATTN
POOL_GROUP_C = B_WIDTH // 4
NA_ROWS = 8
NA_COLS = 16
PEER_HEADS = 8
PEER_KEYS = 128
PEER_EXPERTS = PEER_KEYS * PEER_KEYS
PEER_TOPK = 16
PEER_HALF = 128
IN_COLS = 3 * A_WIDTH + B_WIDTH + 3 * C_WIDTH
N_MOD = 6
EPS = 1e-6
NEG_INF = -1e30
ATTN_SCALE = HEAD_DIM ** -0.5

LANES = 128
SUBLANES = 8
MOD_ROWS = 8
VMEM_LIMIT = 56 * 1024 * 1024

TOK_BLK = 512
MIX_BLK = 2048
GATE_BLK = 256
GATE_PITCH = PEER_KEYS + SUBLANES
GATE_UNROLL = 64
ATTN_UNROLL = 4
EXP_TOK = 1024
EXP_BLK = 1024
EXP_PLANES = EXP_BLK // PEER_KEYS


def _cparams(sem):
    return pltpu.CompilerParams(dimension_semantics=sem, vmem_limit_bytes=VMEM_LIMIT)


def _rms(x, g):
    return x * lax.rsqrt(jnp.mean(x * x, axis=-1, keepdims=True) + EPS) * g


def _mod_row(i, blk, n_ctx_tok, dec_seq):
    start = i * blk
    return jnp.where(start < n_ctx_tok, 0, 1 + jnp.maximum(start - n_ctx_tok, 0) // dec_seq)


def _mods_kernel(c_ref, w_ref, b_ref, o_ref):
    c = c_ref[...]
    s = (c * jax.nn.sigmoid(c)).astype(BF16)
    o_ref[0] = jnp.dot(s, w_ref[0].astype(BF16), preferred_element_type=F32) + b_ref[0]


def _mods_call(cvec, w_ada, b_ada):
    depth, d, n = w_ada.shape
    tn = 1536
    return pl.pallas_call(
        _mods_kernel,
        grid=(depth, n // tn),
        in_specs=[
            pl.BlockSpec((MOD_ROWS, d), lambda l, j: (0, 0)),
            pl.BlockSpec((1, d, tn), lambda l, j: (l, 0, j)),
            pl.BlockSpec((1, 1, tn), lambda l, j: (l, 0, j)),
        ],
        out_specs=pl.BlockSpec((1, MOD_ROWS, tn), lambda l, j: (l, 0, j)),
        out_shape=jax.ShapeDtypeStruct((depth, MOD_ROWS, n), F32),
        compiler_params=_cparams(("arbitrary", "arbitrary")),
        name="mods",
    )(cvec, w_ada, b_ada.reshape(depth, 1, n))


def _w2_kernel(k_ref, wq_ref, o_ref):
    o_ref[0, 0, 0] = lax.dot_general(
        k_ref[0, 0, 0], wq_ref[0], (((1,), (1,)), ((), ())),
        precision=lax.Precision.HIGHEST, preferred_element_type=F32).astype(BF16)


def _w2_call(peer_wq, peer_subkeys):
    depth, d, _ = peer_wq.shape
    return pl.pallas_call(
        _w2_kernel,
        grid=(depth, 2, PEER_HEADS),
        in_specs=[
            pl.BlockSpec((1, 1, 1, PEER_KEYS, PEER_HALF), lambda l, p, h: (l, h, p, 0, 0)),
            pl.BlockSpec((1, d, PEER_HALF), lambda l, p, h: (l, 0, 2 * h + p)),
        ],
        out_specs=pl.BlockSpec((1, 1, 1, PEER_KEYS, d), lambda l, p, h: (l, p, h, 0, 0)),
        out_shape=jax.ShapeDtypeStruct((depth, 2, PEER_HEADS, PEER_KEYS, d), BF16),
        compiler_params=_cparams(("arbitrary",) * 3),
        name="peer_w2",
    )(peer_subkeys, peer_wq)


def _bias_kernel(rpb_ref, o_ref):
    l = pl.program_id(0)
    h = pl.program_id(1)
    n_dr = 2 * NA_ROWS - 1
    n_dc = 2 * NA_COLS - 1
    base = (l * N_HEADS_ATTN + h) * (n_dr * n_dc)
    c = lax.broadcasted_iota(jnp.int32, (GRID_W, LANES), 0)
    lane = lax.broadcasted_iota(jnp.int32, (GRID_W, LANES), 1)
    kc = lane & (GRID_W - 1)
    second = lane >= GRID_W
    diff = kc - c + (NA_COLS - 1)
    cs = jnp.clip(c - NA_COLS // 2, 0, GRID_W - NA_COLS)
    valid = (kc >= cs) & (kc < cs + NA_COLS)
    tiles = []
    for dr in range(n_dr - 1):
        t = jnp.zeros((GRID_W, LANES), F32)
        for dc in range(n_dc):
            hit = diff == dc
            t = jnp.where(hit & ~second, rpb_ref[base + dr * n_dc + dc], t)
            t = jnp.where(hit & second, rpb_ref[base + (dr + 1) * n_dc + dc], t)
        tiles.append(jnp.where(valid, t, NEG_INF))
    for v in range(NA_ROWS):
        for m in range(NA_ROWS // 2):
            o_ref[0, 0, v, :, m * LANES:(m + 1) * LANES] = tiles[2 * m - v + NA_ROWS - 1]


def _bias_call(rpb):
    depth = rpb.shape[0]
    return pl.pallas_call(
        _bias_kernel,
        grid=(depth, N_HEADS_ATTN),
        in_specs=[pl.BlockSpec(memory_space=pltpu.SMEM)],
        out_specs=pl.BlockSpec((1, 1, NA_ROWS, GRID_W, NA_ROWS * GRID_W), lambda l, h: (l, h, 0, 0, 0)),
        out_shape=jax.ShapeDtypeStruct((depth, N_HEADS_ATTN, NA_ROWS, GRID_W, NA_ROWS * GRID_W), F32),
        compiler_params=_cparams(("arbitrary", "arbitrary")),
        name="na_bias",
    )(rpb.reshape(-1))


def _in_kernel(x_ref, m_ref, g_ref, w_ref, o_ref, *, n_ctx_tok, dec_seq):
    d = D_MODEL
    row = _mod_row(pl.program_id(0), x_ref.shape[0], n_ctx_tok, dec_seq)
    sh = m_ref[0, pl.ds(row, 1), 0:d]
    sc = m_ref[0, pl.ds(row, 1), d:2 * d]
    h = _rms(x_ref[...], g_ref[0]) * (1.0 + sc) + sh
    o_ref[...] = jnp.dot(h.astype(BF16), w_ref[0], preferred_element_type=F32)


def _in_call(x, mods, g1, w_in, l, n_ctx_tok, dec_seq):
    t, d = x.shape
    n = w_in.shape[-1]
    return pl.pallas_call(
        functools.partial(_in_kernel, n_ctx_tok=n_ctx_tok, dec_seq=dec_seq),
        grid=(t // TOK_BLK,),
        in_specs=[
            pl.BlockSpec((TOK_BLK, d), lambda i: (i, 0)),
            pl.BlockSpec((1, MOD_ROWS, N_MOD * d), lambda i: (l, 0, 0)),
            pl.BlockSpec((1, 1, d), lambda i: (l, 0, 0)),
            pl.BlockSpec((1, d, n), lambda i: (l, 0, 0)),
        ],
        out_specs=pl.BlockSpec((TOK_BLK, n), lambda i: (i, 0)),
        out_shape=jax.ShapeDtypeStruct((t, n), F32),
        compiler_params=_cparams(("arbitrary",)),
        name="in_proj",
    )(x, mods, g1, w_in)


def _mix_kernel(p_ref, cw_ref, bd_ref, ps_ref, o_ref, *, n_ctx_blk, seq_c, seq_l):
    blk = p_ref.shape[0]
    lseq = jnp.where(pl.program_id(0) < n_ctx_blk, seq_c, seq_l)
    rows = lax.broadcasted_iota(jnp.int32, (blk, A_WIDTH), 0)
    lane = lax.broadcasted_iota(jnp.int32, (blk, A_WIDTH), 1)
    tpos = rows & (lseq - 1)

    def shift_dn(x, m):
        return jnp.where(tpos >= m, pltpu.roll(x, m, 0), 0.0)

    def shift_up(x, m):
        return jnp.where(tpos + m < lseq, pltpu.roll(x, blk - m, 0), 0.0)

    a_in = p_ref[:, 0:A_WIDTH]
    a_b = p_ref[:, A_WIDTH:2 * A_WIDTH]
    a_c = p_ref[:, 2 * A_WIDTH:3 * A_WIDTH]
    u = a_c * a_in
    cw = cw_ref[0]
    y = cw[0:1] * shift_dn(u, 1) + cw[1:2] * u + cw[2:3] * shift_up(u, 1)
    o_ref[:, 0:A_WIDTH] = a_b * y

    p = p_ref[:, 3 * A_WIDTH:3 * A_WIDTH + B_WIDTH]
    f1 = p
    f2 = f1 + shift_up(f1, 1)
    f4 = f2 + shift_up(f2, 2)
    f8 = f4 + shift_up(f4, 4)
    b1 = shift_dn(p, 1)
    b2 = b1 + shift_dn(b1, 1)
    b4 = b2 + shift_dn(b2, 2)
    b8 = b4 + shift_dn(b4, 4)
    grp = jnp.right_shift(lane, POOL_GROUP_C.bit_length() - 1)
    wsum = jnp.where(grp == 0, b1 + f1, jnp.where(grp == 1, b2 + f2, jnp.where(grp == 2, b4 + f4, b8 + f8)))
    half = jnp.left_shift(1, grp)
    lo = jnp.maximum(tpos - half, 0)
    hi = jnp.minimum(tpos + half - 1, lseq - 1)
    cnt = (hi - lo + 1).astype(F32)
    dlt = wsum / cnt - p
    o_ref[:, A_WIDTH:A_WIDTH + B_WIDTH] = (
        jnp.dot(dlt.astype(BF16), bd_ref[0], preferred_element_type=F32) * ps_ref[0])


def _mix_call(proj, conv_w, pool_bd, pool_scale, l, n_ctx_tok, seq_c, seq_l):
    t = proj.shape[0]
    return pl.pallas_call(
        functools.partial(_mix_kernel, n_ctx_blk=n_ctx_tok // MIX_BLK, seq_c=seq_c, seq_l=seq_l),
        grid=(t // MIX_BLK,),
        in_specs=[
            pl.BlockSpec((MIX_BLK, 3 * A_WIDTH + B_WIDTH), lambda i: (i, 0)),
            pl.BlockSpec((1, 3, A_WIDTH), lambda i: (l, 0, 0)),
            pl.BlockSpec((1, B_WIDTH, B_WIDTH), lambda i: (l, 0, 0)),
            pl.BlockSpec((1, 1, B_WIDTH), lambda i: (l, 0, 0)),
        ],
        out_specs=pl.BlockSpec((MIX_BLK, A_WIDTH + B_WIDTH), lambda i: (i, 0)),
        out_shape=jax.ShapeDtypeStruct((t, A_WIDTH + B_WIDTH), F32),
        compiler_params=_cparams(("arbitrary",)),
        name="conv_pool",
    )(proj, conv_w, pool_bd, pool_scale)


def _attn_ctx_kernel(q_ref, k_ref, v_ref, kin_ref, vin_ref, y_ref, ko_ref, vo_ref):
    del kin_ref, vin_ref
    q = q_ref[...]
    k = k_ref[...]
    v = v_ref[...]
    lane = lax.broadcasted_iota(jnp.int32, q.shape, 1)
    kb = k.astype(BF16)
    vb = v.astype(BF16)
    outs = []
    for h in range(2):
        qh = jnp.where((lane >= HEAD_DIM) == (h == 1), q, 0.0).astype(BF16)
        s = lax.dot_general(qh, kb, (((1,), (1,)), ((), ())), preferred_element_type=F32) * ATTN_SCALE
        e = jnp.exp(s - jnp.max(s, axis=-1, keepdims=True))
        pr = e / jnp.sum(e, axis=-1, keepdims=True)
        outs.append(jnp.dot(pr.astype(BF16), vb, preferred_element_type=F32))
        ko_ref[0, 0, h] = k[:, h * HEAD_DIM:(h + 1) * HEAD_DIM]
        vo_ref[0, 0, h] = v[:, h * HEAD_DIM:(h + 1) * HEAD_DIM]
    y_ref[...] = jnp.where(lane < HEAD_DIM, outs[0], outs[1])


def _attn_ctx_call(proj, k_new, v_new, l, n_batch, seq):
    col0 = (3 * A_WIDTH + B_WIDTH) // LANES
    ncol = C_WIDTH // LANES
    kv_shape = jax.ShapeDtypeStruct(k_new.shape, F32)
    kv_spec = pl.BlockSpec((1, 1, 2, seq, HEAD_DIM), lambda b, hp: (b, l, hp, 0, 0))
    return pl.pallas_call(
        _attn_ctx_kernel,
        grid=(n_batch, ncol),
        in_specs=[
            pl.BlockSpec((seq, LANES), lambda b, hp: (b, col0 + hp)),
            pl.BlockSpec((seq, LANES), lambda b, hp: (b, col0 + ncol + hp)),
            pl.BlockSpec((seq, LANES), lambda b, hp: (b, col0 + 2 * ncol + hp)),
            pl.BlockSpec(memory_space=pl.ANY),
            pl.BlockSpec(memory_space=pl.ANY),
        ],
        out_specs=[pl.BlockSpec((seq, LANES), lambda b, hp: (b, hp)), kv_spec, kv_spec],
        out_shape=[jax.ShapeDtypeStruct((n_batch * seq, C_WIDTH), F32), kv_shape, kv_shape],
        input_output_aliases={3: 1, 4: 2},
        compiler_params=_cparams(("arbitrary", "arbitrary")),
        name="attn_ctx",
    )(proj, proj, proj, k_new, v_new)


def _attn_lat_kernel(q_ref, k_ref, v_ref, ck_ref, cv_ref, bv_ref, y_ref, s_ref, e_ref, oc_ref, zi_ref):
    n = q_ref.shape[0]
    rows = n // GRID_W
    nrow = min(NA_ROWS, rows)
    nwin = nrow * GRID_W
    dn = (((1,), (1,)), ((), ()))
    ck = ck_ref[0, 0].astype(BF16)
    cv = cv_ref[0, 0].astype(BF16)
    lane_all = lax.broadcasted_iota(jnp.int32, (n, LANES), 1)
    lane = lax.broadcasted_iota(jnp.int32, (GRID_W, LANES), 1)

    for h in range(2):
        second = h == 1

        def scores(r, carry):
            rs = jnp.clip(r - nrow // 2, 0, rows - nrow)
            qrow = pl.ds(pl.multiple_of(r * GRID_W, GRID_W), GRID_W)
            qh = jnp.where((lane >= HEAD_DIM) == second, q_ref[qrow, :], 0.0).astype(BF16)
            kw = k_ref[pl.ds(pl.multiple_of(rs * GRID_W, GRID_W), nwin), :].astype(BF16)
            s_ref[qrow, :] = (lax.dot_general(qh, kw, dn, preferred_element_type=F32) * ATTN_SCALE
                              + bv_ref[0, h, r - rs])
            return carry

        lax.fori_loop(0, rows, scores, 0, unroll=ATTN_UNROLL)

        qh_all = jnp.where((lane_all >= HEAD_DIM) == second, q_ref[...], 0.0).astype(BF16)
        s_c = lax.dot_general(qh_all, ck, dn, preferred_element_type=F32) * ATTN_SCALE
        s_w = s_ref[...]
        mx = jnp.maximum(jnp.max(s_w, axis=-1, keepdims=True), jnp.max(s_c, axis=-1, keepdims=True))
        e_w = jnp.exp(s_w - mx)
        e_c = jnp.exp(s_c - mx)
        z = jnp.sum(e_w, axis=-1, keepdims=True) + jnp.sum(e_c, axis=-1, keepdims=True)
        e_ref[...] = e_w.astype(BF16)
        oc_ref[...] = jnp.dot(e_c.astype(BF16), cv, preferred_element_type=F32)
        zi_ref[...] = jnp.broadcast_to(1.0 / z, (n, LANES))

        def values(r, carry):
            rs = jnp.clip(r - nrow // 2, 0, rows - nrow)
            qrow = pl.ds(pl.multiple_of(r * GRID_W, GRID_W), GRID_W)
            vw = v_ref[pl.ds(pl.multiple_of(rs * GRID_W, GRID_W), nwin), :].astype(BF16)
            o = (jnp.dot(e_ref[qrow, :], vw, preferred_element_type=F32) + oc_ref[qrow, :]) * zi_ref[qrow, :]
            if second:
                o = jnp.where(lane < HEAD_DIM, y_ref[qrow, :], o)
            y_ref[qrow, :] = o
            return carry

        lax.fori_loop(0, rows, values, 0, unroll=ATTN_UNROLL)


def _attn_lat_call(proj, ck, cv, bias, l, n_ctx_tok, n_batch, seq):
    col0 = (3 * A_WIDTH + B_WIDTH) // LANES
    ncol = C_WIDTH // LANES
    row0 = n_ctx_tok // seq
    past = ck.shape[2]
    return pl.pallas_call(
        _attn_lat_kernel,
        grid=(n_batch, ncol),
        in_specs=[
            pl.BlockSpec((seq, LANES), lambda b, hp: (row0 + b, col0 + hp)),
            pl.BlockSpec((seq, LANES), lambda b, hp: (row0 + b, col0 + ncol + hp)),
            pl.BlockSpec((seq, LANES), lambda b, hp: (row0 + b, col0 + 2 * ncol + hp)),
            pl.BlockSpec((1, 1, past, LANES), lambda b, hp: (b, l, 0, hp)),
            pl.BlockSpec((1, 1, past, LANES), lambda b, hp: (b, l, 0, hp)),
            pl.BlockSpec((1, 2, NA_ROWS, GRID_W, NA_ROWS * GRID_W), lambda b, hp: (l, hp, 0, 0, 0)),
        ],
        out_specs=pl.BlockSpec((seq, LANES), lambda b, hp: (b, hp)),
        out_shape=jax.ShapeDtypeStruct((n_batch * seq, C_WIDTH), F32),
        scratch_shapes=[pltpu.VMEM((seq, NA_ROWS * GRID_W), F32), pltpu.VMEM((seq, NA_ROWS * GRID_W), BF16),
                        pltpu.VMEM((seq, LANES), F32), pltpu.VMEM((seq, LANES), F32)],
        compiler_params=_cparams(("arbitrary", "arbitrary")),
        name="attn_lat",
    )(proj, proj, proj, ck, cv, bias)


def _out_kernel(ycp_ref, yat_ref, x_ref, m_ref, g_ref, w_ref, xo_ref, h_ref, *, n_ctx_tok, dec_seq):
    d = D_MODEL
    half = A_WIDTH + B_WIDTH
    row = _mod_row(pl.program_id(0), x_ref.shape[0], n_ctx_tok, dec_seq)
    y = (jnp.dot(ycp_ref[...].astype(BF16), w_ref[0, 0:half], preferred_element_type=F32)
         + jnp.dot(yat_ref[...].astype(BF16), w_ref[0, half:2 * half], preferred_element_type=F32))
    x = x_ref[...] + m_ref[0, pl.ds(row, 1), 2 * d:3 * d] * y
    xo_ref[...] = x
    sh = m_ref[0, pl.ds(row, 1), 3 * d:4 * d]
    sc = m_ref[0, pl.ds(row, 1), 4 * d:5 * d]
    h_ref[...] = (_rms(x, g_ref[0]) * (1.0 + sc) + sh).astype(BF16)


def _out_call(ycp, yat, x, mods, g2, w_out, l, n_ctx_tok, dec_seq):
    t, d = x.shape
    half = ycp.shape[1]
    return pl.pallas_call(
        functools.partial(_out_kernel, n_ctx_tok=n_ctx_tok, dec_seq=dec_seq),
        grid=(t // TOK_BLK,),
        in_specs=[
            pl.BlockSpec((TOK_BLK, half), lambda i: (i, 0)),
            pl.BlockSpec((TOK_BLK, half), lambda i: (i, 0)),
            pl.BlockSpec((TOK_BLK, d), lambda i: (i, 0)),
            pl.BlockSpec((1, MOD_ROWS, N_MOD * d), lambda i: (l, 0, 0)),
            pl.BlockSpec((1, 1, d), lambda i: (l, 0, 0)),
            pl.BlockSpec((1, d, d), lambda i: (l, 0, 0)),
        ],
        out_specs=[pl.BlockSpec((TOK_BLK, d), lambda i: (i, 0)), pl.BlockSpec((TOK_BLK, d), lambda i: (i, 0))],
        out_shape=[jax.ShapeDtypeStruct((t, d), F32), jax.ShapeDtypeStruct((t, d), BF16)],
        compiler_params=_cparams(("arbitrary",)),
        name="out_proj",
    )(ycp, yat, x, mods, g2, w_out)


def _oem_sort_pairs(n):
    pairs = []

    def merge(lo, m, r):
        step = r * 2
        if step < m:
            merge(lo, m, step)
            merge(lo + r, m, step)
            pairs.extend((i, i + r) for i in range(lo + r, lo + m - r, step))
        else:
            pairs.append((lo, lo + r))

    def sort(lo, m):
        if m > 1:
            sort(lo, m // 2)
            sort(lo + m // 2, m // 2)
            merge(lo, m, 1)

    sort(0, n)
    return pairs


_SORT16 = _oem_sort_pairs(PEER_TOPK)
_CAND = [(a, b) for a in range(PEER_TOPK) for b in range(PEER_TOPK) if (a + 1) * (b + 1) <= PEER_TOPK]


def _cmp_exchange(vals, idxs, i, j):
    first = vals[i] >= vals[j]
    vals[i], vals[j] = jnp.maximum(vals[i], vals[j]), jnp.minimum(vals[i], vals[j])
    idxs[i], idxs[j] = jnp.where(first, idxs[i], idxs[j]), jnp.where(first, idxs[j], idxs[i])


def _top16(load):
    k = PEER_TOPK
    best = None
    for g in range(PEER_KEYS // k):
        vals = [load(g * k + i) for i in range(k)]
        idxs = [jnp.full(vals[0].shape, float(g * k + i), F32) for i in range(k)]
        for i, j in _SORT16:
            _cmp_exchange(vals, idxs, i, j)
        if best is not None:
            bv, bi = best
            for i in range(k):
                first = bv[i] >= vals[k - 1 - i]
                bi[i] = jnp.where(first, bi[i], idxs[k - 1 - i])
                bv[i] = jnp.maximum(bv[i], vals[k - 1 - i])
            stride = k // 2
            while stride >= 1:
                for i in range(k):
                    if i & stride == 0:
                        _cmp_exchange(bv, bi, i, i + stride)
                stride //= 2
            vals, idxs = bv, bi
        best = (vals, idxs)
    return best


def _gate_kernel(h_ref, w2_ref, o_ref, sc_ref, i1_ref, i2_ref, g_ref, gs_ref):
    nk = PEER_KEYS
    k = PEER_TOPK
    sc_ref[...] = lax.dot_general(w2_ref[0], h_ref[...], (((1,), (1,)), ((), ())), preferred_element_type=F32)
    for sb in range(GATE_BLK // LANES):
        lanes = slice(sb * LANES, (sb + 1) * LANES)
        tops = [_top16(lambda key, p=p, lanes=lanes: sc_ref[pl.ds((p * nk + key) * PEER_HEADS, PEER_HEADS), lanes])
                for p in range(2)]
        (v1, x1), (v2, x2) = tops
        cand = [v1[a] + v2[b] for a, b in _CAND]
        slot_v, slot_1, slot_2 = [], [], []
        for it in range(k):
            live = [n for n, (a, b) in enumerate(_CAND) if (a + 1) * (b + 1) <= it + 1]
            mx = functools.reduce(jnp.maximum, [cand[n] for n in live])
            found = jnp.zeros(mx.shape, jnp.bool_)
            s1 = jnp.zeros(mx.shape, F32)
            s2 = jnp.zeros(mx.shape, F32)
            for n in live:
                a, b = _CAND[n]
                eq = cand[n] == mx
                sel = eq & ~found
                found = found | eq
                s1 = jnp.where(sel, x1[a], s1)
                s2 = jnp.where(sel, x2[b], s2)
                cand[n] = jnp.where(sel, -jnp.inf, cand[n])
            slot_v.append(mx)
            slot_1.append(s1)
            slot_2.append(s2)
        e = [jnp.exp(v - slot_v[0]) for v in slot_v]
        z = functools.reduce(jnp.add, e)
        gates = [x / z for x in e]
        i1_ref[lanes, :] = jnp.concatenate(slot_1, axis=0).T
        i2_ref[lanes, :] = jnp.concatenate(slot_2, axis=0).T
        g_ref[lanes, :] = jnp.concatenate(gates, axis=0).T
    pk = 2 * SUBLANES
    key_iota = lax.broadcasted_iota(jnp.int32, (nk // pk, pk, nk), 0) * pk + lax.broadcasted_iota(
        jnp.int32, (nk // pk, pk, nk), 1)
    key_iota = key_iota.astype(F32).astype(BF16)
    one = jnp.ones((), BF16)
    zero = jnp.zeros((), BF16)

    def row(ref, t):
        return jnp.broadcast_to(ref[pl.ds(t, 1), :], (pk, nk)).astype(BF16)[None]

    def per_token(t, carry):
        lt = jnp.where(key_iota == row(i1_ref, t), one, zero).reshape(nk, nk)
        rt = jnp.where(key_iota == row(i2_ref, t), row(g_ref, t), zero).reshape(nk, nk)
        gt = lax.dot_general(lt, rt, (((1,), (1,)), ((), ())), preferred_element_type=F32)
        gs_ref[pl.ds(pl.multiple_of(t * GATE_PITCH, SUBLANES), nk), :] = gt
        return carry

    lax.fori_loop(0, GATE_BLK, per_token, 0, unroll=GATE_UNROLL)
    for i1 in range(nk):
        o_ref[i1] = gs_ref[pl.ds(i1, GATE_BLK, stride=GATE_PITCH), :].astype(BF16)


def _gate_call(h2, w2t, l):
    t, d = h2.shape
    nrow = w2t.shape[1]
    return pl.pallas_call(
        _gate_kernel,
        grid=(t // GATE_BLK,),
        in_specs=[
            pl.BlockSpec((GATE_BLK, d), lambda i: (i, 0)),
            pl.BlockSpec((1, nrow, d), lambda i: (l, 0, 0)),
        ],
        out_specs=pl.BlockSpec((PEER_KEYS, GATE_BLK, PEER_KEYS), lambda i: (0, i, 0)),
        out_shape=jax.ShapeDtypeStruct((PEER_KEYS, t, PEER_KEYS), BF16),
        scratch_shapes=[
            pltpu.VMEM((nrow, GATE_BLK), F32),
            pltpu.VMEM((GATE_BLK, PEER_HEADS * PEER_TOPK), F32),
            pltpu.VMEM((GATE_BLK, PEER_HEADS * PEER_TOPK), F32),
            pltpu.VMEM((GATE_BLK, PEER_HEADS * PEER_TOPK), F32),
            pltpu.VMEM((GATE_BLK * GATE_PITCH, PEER_KEYS), F32),
        ],
        compiler_params=_cparams(("arbitrary",)),
        name="peer_gates",
    )(h2, w2t)


def _gelu(x):
    return 0.5 * x * (1.0 + jnp.tanh(0.7978845608028654 * (x + 0.044715 * (x * x * x))))


def _expert_kernel(h_ref, u_ref, v_ref, g_ref, x_ref, m_ref, o_ref, acc_ref, *, n_ctx_tok, dec_seq):
    d = D_MODEL
    j = pl.program_id(1)

    @pl.when(j == 0)
    def _():
        acc_ref[...] = jnp.zeros_like(acc_ref)

    a = lax.dot_general(h_ref[...], u_ref[0].astype(BF16), (((1,), (1,)), ((), ())), preferred_element_type=F32)
    hid = jnp.concatenate(
        [(_gelu(a[:, q * PEER_KEYS:(q + 1) * PEER_KEYS]) * g_ref[q].astype(F32)).astype(BF16)
         for q in range(EXP_PLANES)], axis=1)
    acc_ref[...] += jnp.dot(hid, v_ref[0].astype(BF16), preferred_element_type=F32)

    @pl.when(j == pl.num_programs(1) - 1)
    def _():
        row = _mod_row(pl.program_id(0), x_ref.shape[0], n_ctx_tok, dec_seq)
        o_ref[...] = x_ref[...] + m_ref[0, pl.ds(row, 1), 5 * d:6 * d] * acc_ref[...]


def _expert_call(h2, u, v, gates, x, mods, l, n_ctx_tok, dec_seq):
    t, d = x.shape
    ne = u.shape[1]
    return pl.pallas_call(
        functools.partial(_expert_kernel, n_ctx_tok=n_ctx_tok, dec_seq=dec_seq),
        grid=(t // EXP_TOK, ne // EXP_BLK),
        in_specs=[
            pl.BlockSpec((EXP_TOK, d), lambda i, j: (i, 0)),
            pl.BlockSpec((1, EXP_BLK, d), lambda i, j: (l, j, 0)),
            pl.BlockSpec((1, EXP_BLK, d), lambda i, j: (l, j, 0)),
            pl.BlockSpec((EXP_PLANES, EXP_TOK, PEER_KEYS), lambda i, j: (j, i, 0)),
            pl.BlockSpec((EXP_TOK, d), lambda i, j: (i, 0)),
            pl.BlockSpec((1, MOD_ROWS, N_MOD * d), lambda i, j: (l, 0, 0)),
        ],
        out_specs=pl.BlockSpec((EXP_TOK, d), lambda i, j: (i, 0)),
        out_shape=jax.ShapeDtypeStruct((t, d), F32),
        scratch_shapes=[pltpu.VMEM((EXP_TOK, d), F32)],
        compiler_params=_cparams(("arbitrary", "arbitrary")),
        name="peer_experts",
    )(h2, u, v, gates, x, mods)


def _final_kernel(x_ref, g_ref, o_ref):
    o_ref[...] = _rms(x_ref[...], g_ref[...])


def _final_call(x, g, row0, n_rows):
    d = x.shape[1]
    blk0 = row0 // TOK_BLK
    return pl.pallas_call(
        _final_kernel,
        grid=(n_rows // TOK_BLK,),
        in_specs=[pl.BlockSpec((TOK_BLK, d), lambda i: (blk0 + i, 0)), pl.BlockSpec((1, d), lambda i: (0, 0))],
        out_specs=pl.BlockSpec((TOK_BLK, d), lambda i: (i, 0)),
        out_shape=jax.ShapeDtypeStruct((n_rows, d), F32),
        compiler_params=_cparams(("arbitrary",)),
        name="final_norm",
    )(x, g.reshape(1, d))


def kernel(x_prompt, x_sample, c, cache_k, cache_v, c_ctx, norm1_g, norm2_g, w_ada, b_ada, w_in, conv_w,
           pool_w, pool_scale, rpb, w_out, peer_wq, peer_subkeys, peer_u, peer_v, final_g):
    n_b, seq_c, d = x_prompt.shape
    n_db, seq_l, _ = x_sample.shape
    depth = w_ada.shape[0]
    n_ctx_tok = n_b * seq_c
    assert d == D_MODEL and seq_l % GRID_W == 0 and seq_l // GRID_W >= NA_ROWS
    assert n_ctx_tok % MIX_BLK == 0 and seq_l % MIX_BLK == 0 and MIX_BLK % seq_c == 0
    assert 1 + n_db <= MOD_ROWS and (n_ctx_tok + n_db * seq_l) % EXP_TOK == 0

    x = jnp.concatenate([x_prompt.reshape(n_ctx_tok, d), x_sample.reshape(n_db * seq_l, d)], axis=0)
    cvec = jnp.concatenate([c_ctx[None, :], c, jnp.zeros((MOD_ROWS - 1 - n_db, d), F32)], axis=0)
    mods = _mods_call(cvec, w_ada, b_ada)
    w2t = _w2_call(peer_wq, peer_subkeys)
    w2t = w2t.transpose(0, 1, 3, 2, 4).reshape(depth, 2 * PEER_KEYS * PEER_HEADS, d)
    bias = _bias_call(rpb)
    past = cache_k.shape[3]
    ck = cache_k.transpose(0, 1, 3, 2, 4).reshape(n_db, depth, past, C_WIDTH)
    cv = cache_v.transpose(0, 1, 3, 2, 4).reshape(n_db, depth, past, C_WIDTH)
    w_in_b = w_in.astype(BF16)
    w_out_b = w_out.astype(BF16)
    eye = jnp.eye(B_WIDTH // POOL_GROUP_C, dtype=F32)
    pool_bd = jnp.einsum('gh,lgcd->lgchd', eye, pool_w).reshape(depth, B_WIDTH, B_WIDTH).astype(BF16)
    g1 = norm1_g.reshape(depth, 1, d)
    g2 = norm2_g.reshape(depth, 1, d)
    ps = pool_scale.reshape(depth, 1, B_WIDTH)

    k_new = jnp.zeros((n_b, depth, N_HEADS_ATTN, seq_c, HEAD_DIM), F32)
    v_new = jnp.zeros((n_b, depth, N_HEADS_ATTN, seq_c, HEAD_DIM), F32)
    for l in range(depth):
        proj = _in_call(x, mods, g1, w_in_b, l, n_ctx_tok, seq_l)
        ycp = _mix_call(proj, conv_w, pool_bd, ps, l, n_ctx_tok, seq_c, seq_l)
        y_c, k_new, v_new = _attn_ctx_call(proj, k_new, v_new, l, n_b, seq_c)
        y_l = _attn_lat_call(proj, ck, cv, bias, l, n_ctx_tok, n_db, seq_l)
        yat = jnp.concatenate([y_c, y_l], axis=0)
        x, h2 = _out_call(ycp, yat, x, mods, g2, w_out_b, l, n_ctx_tok, seq_l)
        gates = _gate_call(h2, w2t, l)
        x = _expert_call(h2, peer_u, peer_v, gates, x, mods, l, n_ctx_tok, seq_l)
    y_prompt = _final_call(x, final_g, 0, n_ctx_tok).reshape(n_b, seq_c, d)
    y_sample = _final_call(x, final_g, n_ctx_tok, n_db * seq_l).reshape(n_db, seq_l, d)
    return (y_prompt, y_sample, k_new, v_new)
```

```python
import functools

import numpy as np
import jax
import jax.numpy as jnp
from jax import lax
from jax.experimental import pallas as pl
from jax.experimental.pallas import tpu as pltpu

F32 = jnp.float32
BF16 = jnp.bfloat16

D_MODEL = 1024
DEPTH = 4
GRID_W = 64
A_WIDTH = D_MODEL // 4
B_WIDTH = D_MODEL // 4
C_WIDTH = D_MODEL // 2
N_HEADS_ATTN = 8
HEAD_DIM = C_WIDTH // N_HEADS_ATTN
POOL_GROUP_C = B_WIDTH // 4
NA_ROWS = 8
NA_COLS = 16
PEER_HEADS = 8
PEER_KEYS = 128
PEER_EXPERTS = PEER_KEYS * PEER_KEYS
PEER_TOPK = 16
PEER_HALF = 128
IN_COLS = 3 * A_WIDTH + B_WIDTH + 3 * C_WIDTH
N_MOD = 6
EPS = 1e-6
NEG_INF = -1e30
ATTN_SCALE = HEAD_DIM ** -0.5

LANES = 128
SUBLANES = 8
MOD_ROWS = 8
VMEM_LIMIT = 56 * 1024 * 1024

TOK_BLK = 512
MIX_BLK = 2048
GATE_BLK = 256
GATE_PITCH = PEER_KEYS + SUBLANES
GATE_UNROLL = 64
ATTN_CTX_COLS = 256
ATTN_UNROLL = 4
EXP_TOK = 1024
EXP_BLK = 1024
EXP_PLANES = EXP_BLK // PEER_KEYS


def _cparams(sem):
    return pltpu.CompilerParams(dimension_semantics=sem, vmem_limit_bytes=VMEM_LIMIT)


def _rms(x, g):
    return x * lax.rsqrt(jnp.mean(x * x, axis=-1, keepdims=True) + EPS) * g


def _mod_row(i, blk, n_ctx_tok, dec_seq):
    start = i * blk
    return jnp.where(start < n_ctx_tok, 0, 1 + jnp.maximum(start - n_ctx_tok, 0) // dec_seq)


def _mods_kernel(c_ref, w_ref, b_ref, o_ref):
    c = c_ref[...]
    s = (c * jax.nn.sigmoid(c)).astype(BF16)
    o_ref[0] = jnp.dot(s, w_ref[0].astype(BF16), preferred_element_type=F32) + b_ref[0]


def _mods_call(cvec, w_ada, b_ada):
    depth, d, n = w_ada.shape
    tn = 1536
    return pl.pallas_call(
        _mods_kernel,
        grid=(depth, n // tn),
        in_specs=[
            pl.BlockSpec((MOD_ROWS, d), lambda l, j: (0, 0)),
            pl.BlockSpec((1, d, tn), lambda l, j: (l, 0, j)),
            pl.BlockSpec((1, 1, tn), lambda l, j: (l, 0, j)),
        ],
        out_specs=pl.BlockSpec((1, MOD_ROWS, tn), lambda l, j: (l, 0, j)),
        out_shape=jax.ShapeDtypeStruct((depth, MOD_ROWS, n), F32),
        compiler_params=_cparams(("arbitrary", "arbitrary")),
        name="mods",
    )(cvec, w_ada, b_ada.reshape(depth, 1, n))


def _w2_kernel(k_ref, wq_ref, o_ref):
    o_ref[0, 0, 0] = lax.dot_general(
        k_ref[0, 0, 0], wq_ref[0], (((1,), (1,)), ((), ())),
        precision=lax.Precision.HIGHEST, preferred_element_type=F32).astype(BF16)


def _w2_call(peer_wq, peer_subkeys):
    depth, d, _ = peer_wq.shape
    return pl.pallas_call(
        _w2_kernel,
        grid=(depth, 2, PEER_HEADS),
        in_specs=[
            pl.BlockSpec((1, 1, 1, PEER_KEYS, PEER_HALF), lambda l, p, h: (l, h, p, 0, 0)),
            pl.BlockSpec((1, d, PEER_HALF), lambda l, p, h: (l, 0, 2 * h + p)),
        ],
        out_specs=pl.BlockSpec((1, 1, 1, PEER_KEYS, d), lambda l, p, h: (l, p, h, 0, 0)),
        out_shape=jax.ShapeDtypeStruct((depth, 2, PEER_HEADS, PEER_KEYS, d), BF16),
        compiler_params=_cparams(("arbitrary",) * 3),
        name="peer_w2",
    )(peer_subkeys, peer_wq)


def _bias_kernel(rpb_ref, o_ref):
    l = pl.program_id(0)
    h = pl.program_id(1)
    n_dr = 2 * NA_ROWS - 1
    n_dc = 2 * NA_COLS - 1
    base = (l * N_HEADS_ATTN + h) * (n_dr * n_dc)
    c = lax.broadcasted_iota(jnp.int32, (GRID_W, LANES), 0)
    lane = lax.broadcasted_iota(jnp.int32, (GRID_W, LANES), 1)
    kc = lane & (GRID_W - 1)
    second = lane >= GRID_W
    diff = kc - c + (NA_COLS - 1)
    cs = jnp.clip(c - NA_COLS // 2, 0, GRID_W - NA_COLS)
    valid = (kc >= cs) & (kc < cs + NA_COLS)
    tiles = []
    for dr in range(n_dr - 1):
        t = jnp.zeros((GRID_W, LANES), F32)
        for dc in range(n_dc):
            hit = diff == dc
            t = jnp.where(hit & ~second, rpb_ref[base + dr * n_dc + dc], t)
            t = jnp.where(hit & second, rpb_ref[base + (dr + 1) * n_dc + dc], t)
        tiles.append(jnp.where(valid, t, NEG_INF))
    for v in range(NA_ROWS):
        for m in range(NA_ROWS // 2):
            o_ref[0, 0, v, :, m * LANES:(m + 1) * LANES] = tiles[2 * m - v + NA_ROWS - 1]


def _bias_call(rpb):
    depth = rpb.shape[0]
    return pl.pallas_call(
        _bias_kernel,
        grid=(depth, N_HEADS_ATTN),
        in_specs=[pl.BlockSpec(memory_space=pltpu.SMEM)],
        out_specs=pl.BlockSpec((1, 1, NA_ROWS, GRID_W, NA_ROWS * GRID_W), lambda l, h: (l, h, 0, 0, 0)),
        out_shape=jax.ShapeDtypeStruct((depth, N_HEADS_ATTN, NA_ROWS, GRID_W, NA_ROWS * GRID_W), F32),
        compiler_params=_cparams(("arbitrary", "arbitrary")),
        name="na_bias",
    )(rpb.reshape(-1))


def _in_kernel(x_ref, m_ref, g_ref, w_ref, o_ref, *, n_ctx_tok, dec_seq):
    d = D_MODEL
    row = _mod_row(pl.program_id(0), x_ref.shape[0], n_ctx_tok, dec_seq)
    sh = m_ref[0, pl.ds(row, 1), 0:d]
    sc = m_ref[0, pl.ds(row, 1), d:2 * d]
    h = _rms(x_ref[...], g_ref[0]) * (1.0 + sc) + sh
    o_ref[...] = jnp.dot(h.astype(BF16), w_ref[0], preferred_element_type=F32)


def _in_call(x, mods, g1, w_in, l, n_ctx_tok, dec_seq):
    t, d = x.shape
    n = w_in.shape[-1]
    return pl.pallas_call(
        functools.partial(_in_kernel, n_ctx_tok=n_ctx_tok, dec_seq=dec_seq),
        grid=(t // TOK_BLK,),
        in_specs=[
            pl.BlockSpec((TOK_BLK, d), lambda i: (i, 0)),
            pl.BlockSpec((1, MOD_ROWS, N_MOD * d), lambda i: (l, 0, 0)),
            pl.BlockSpec((1, 1, d), lambda i: (l, 0, 0)),
            pl.BlockSpec((1, d, n), lambda i: (l, 0, 0)),
        ],
        out_specs=pl.BlockSpec((TOK_BLK, n), lambda i: (i, 0)),
        out_shape=jax.ShapeDtypeStruct((t, n), F32),
        compiler_params=_cparams(("arbitrary",)),
        name="in_proj",
    )(x, mods, g1, w_in)


def _mix_kernel(p_ref, cw_ref, bd_ref, ps_ref, o_ref, *, n_ctx_blk, seq_c, seq_l):
    blk = p_ref.shape[0]
    lseq = jnp.where(pl.program_id(0) < n_ctx_blk, seq_c, seq_l)
    rows = lax.broadcasted_iota(jnp.int32, (blk, A_WIDTH), 0)
    lane = lax.broadcasted_iota(jnp.int32, (blk, A_WIDTH), 1)
    tpos = rows & (lseq - 1)

    def shift_dn(x, m):
        return jnp.where(tpos >= m, pltpu.roll(x, m, 0), 0.0)

    def shift_up(x, m):
        return jnp.where(tpos + m < lseq, pltpu.roll(x, blk - m, 0), 0.0)

    a_in = p_ref[:, 0:A_WIDTH]
    a_b = p_ref[:, A_WIDTH:2 * A_WIDTH]
    a_c = p_ref[:, 2 * A_WIDTH:3 * A_WIDTH]
    u = a_c * a_in
    cw = cw_ref[0]
    y = cw[0:1] * shift_dn(u, 1) + cw[1:2] * u + cw[2:3] * shift_up(u, 1)
    o_ref[:, 0:A_WIDTH] = (a_b * y).astype(o_ref.dtype)

    p = p_ref[:, 3 * A_WIDTH:3 * A_WIDTH + B_WIDTH]
    f1 = p
    f2 = f1 + shift_up(f1, 1)
    f4 = f2 + shift_up(f2, 2)
    f8 = f4 + shift_up(f4, 4)
    b1 = shift_dn(p, 1)
    b2 = b1 + shift_dn(b1, 1)
    b4 = b2 + shift_dn(b2, 2)
    b8 = b4 + shift_dn(b4, 4)
    grp = jnp.right_shift(lane, POOL_GROUP_C.bit_length() - 1)
    wsum = jnp.where(grp == 0, b1 + f1, jnp.where(grp == 1, b2 + f2, jnp.where(grp == 2, b4 + f4, b8 + f8)))
    half = jnp.left_shift(1, grp)
    lo = jnp.maximum(tpos - half, 0)
    hi = jnp.minimum(tpos + half - 1, lseq - 1)
    cnt = (hi - lo + 1).astype(F32)
    dlt = wsum / cnt - p
    o_ref[:, A_WIDTH:A_WIDTH + B_WIDTH] = (
        jnp.dot(dlt.astype(BF16), bd_ref[0], preferred_element_type=F32) * ps_ref[0]).astype(o_ref.dtype)


def _mix_call(proj, conv_w, pool_bd, pool_scale, l, n_ctx_tok, seq_c, seq_l):
    t = proj.shape[0]
    return pl.pallas_call(
        functools.partial(_mix_kernel, n_ctx_blk=n_ctx_tok // MIX_BLK, seq_c=seq_c, seq_l=seq_l),
        grid=(t // MIX_BLK,),
        in_specs=[
            pl.BlockSpec((MIX_BLK, 3 * A_WIDTH + B_WIDTH), lambda i: (i, 0)),
            pl.BlockSpec((1, 3, A_WIDTH), lambda i: (l, 0, 0)),
            pl.BlockSpec((1, B_WIDTH, B_WIDTH), lambda i: (l, 0, 0)),
            pl.BlockSpec((1, 1, B_WIDTH), lambda i: (l, 0, 0)),
        ],
        out_specs=pl.BlockSpec((MIX_BLK, A_WIDTH + B_WIDTH), lambda i: (i, 0)),
        out_shape=jax.ShapeDtypeStruct((t, A_WIDTH + B_WIDTH), BF16),
        compiler_params=_cparams(("arbitrary",)),
        name="conv_pool",
    )(proj, conv_w, pool_bd, pool_scale)


def _attn_ctx_kernel(q_ref, k_ref, v_ref, kin_ref, vin_ref, y_ref, ko_ref, vo_ref):
    del kin_ref, vin_ref
    q = q_ref[...]
    k = k_ref[...]
    v = v_ref[...]
    head = jnp.right_shift(lax.broadcasted_iota(jnp.int32, q.shape, 1), HEAD_DIM.bit_length() - 1)
    kb = k.astype(BF16)
    vb = v.astype(BF16)
    y = jnp.zeros(q.shape, F32)
    for h in range(q.shape[1] // HEAD_DIM):
        qh = jnp.where(head == h, q, 0.0).astype(BF16)
        s = lax.dot_general(qh, kb, (((1,), (1,)), ((), ())), preferred_element_type=F32) * ATTN_SCALE
        e = jnp.exp(s - jnp.max(s, axis=-1, keepdims=True))
        pr = e / jnp.sum(e, axis=-1, keepdims=True)
        y = jnp.where(head == h, jnp.dot(pr.astype(BF16), vb, preferred_element_type=F32), y)
        ko_ref[0, 0, h] = k[:, h * HEAD_DIM:(h + 1) * HEAD_DIM]
        vo_ref[0, 0, h] = v[:, h * HEAD_DIM:(h + 1) * HEAD_DIM]
    y_ref[...] = y.astype(y_ref.dtype)


def _attn_ctx_call(proj, k_new, v_new, l, n_batch, seq):
    width = ATTN_CTX_COLS
    col0 = (3 * A_WIDTH + B_WIDTH) // width
    ncol = C_WIDTH // width
    kv_shape = jax.ShapeDtypeStruct(k_new.shape, F32)
    kv_spec = pl.BlockSpec((1, 1, width // HEAD_DIM, seq, HEAD_DIM), lambda b, hp: (b, l, hp, 0, 0))
    return pl.pallas_call(
        _attn_ctx_kernel,
        grid=(n_batch, ncol),
        in_specs=[
            pl.BlockSpec((seq, width), lambda b, hp: (b, col0 + hp)),
            pl.BlockSpec((seq, width), lambda b, hp: (b, col0 + ncol + hp)),
            pl.BlockSpec((seq, width), lambda b, hp: (b, col0 + 2 * ncol + hp)),
            pl.BlockSpec(memory_space=pl.ANY),
            pl.BlockSpec(memory_space=pl.ANY),
        ],
        out_specs=[pl.BlockSpec((seq, width), lambda b, hp: (b, hp)), kv_spec, kv_spec],
        out_shape=[jax.ShapeDtypeStruct((n_batch * seq, C_WIDTH), BF16), kv_shape, kv_shape],
        input_output_aliases={3: 1, 4: 2},
        compiler_params=_cparams(("arbitrary", "arbitrary")),
        name="attn_ctx",
    )(proj, proj, proj, k_new, v_new)


def _attn_lat_kernel(q_ref, k_ref, v_ref, ck_ref, cv_ref, bv_ref, y_ref, s_ref, e_ref, oc_ref, zi_ref):
    n = q_ref.shape[0]
    rows = n // GRID_W
    nrow = min(NA_ROWS, rows)
    nwin = nrow * GRID_W
    dn = (((1,), (1,)), ((), ()))
    ck = ck_ref[0, 0].astype(BF16)
    cv = cv_ref[0, 0].astype(BF16)
    lane_all = lax.broadcasted_iota(jnp.int32, (n, LANES), 1)
    lane = lax.broadcasted_iota(jnp.int32, (GRID_W, LANES), 1)

    for h in range(2):
        second = h == 1

        def scores(r, carry):
            rs = jnp.clip(r - nrow // 2, 0, rows - nrow)
            qrow = pl.ds(pl.multiple_of(r * GRID_W, GRID_W), GRID_W)
            qh = jnp.where((lane >= HEAD_DIM) == second, q_ref[qrow, :], 0.0).astype(BF16)
            kw = k_ref[pl.ds(pl.multiple_of(rs * GRID_W, GRID_W), nwin), :].astype(BF16)
            s_ref[qrow, :] = (lax.dot_general(qh, kw, dn, preferred_element_type=F32) * ATTN_SCALE
                              + bv_ref[0, h, r - rs])
            return carry

        lax.fori_loop(0, rows, scores, 0, unroll=ATTN_UNROLL)

        qh_all = jnp.where((lane_all >= HEAD_DIM) == second, q_ref[...], 0.0).astype(BF16)
        s_c = lax.dot_general(qh_all, ck, dn, preferred_element_type=F32) * ATTN_SCALE
        s_w = s_ref[...]
        mx = jnp.maximum(jnp.max(s_w, axis=-1, keepdims=True), jnp.max(s_c, axis=-1, keepdims=True))
        e_w = jnp.exp(s_w - mx)
        e_c = jnp.exp(s_c - mx)
        z = jnp.sum(e_w, axis=-1, keepdims=True) + jnp.sum(e_c, axis=-1, keepdims=True)
        e_ref[...] = e_w.astype(BF16)
        oc_ref[...] = jnp.dot(e_c.astype(BF16), cv, preferred_element_type=F32)
        zi_ref[...] = jnp.broadcast_to(1.0 / z, (n, LANES))

        def values(r, carry):
            rs = jnp.clip(r - nrow // 2, 0, rows - nrow)
            qrow = pl.ds(pl.multiple_of(r * GRID_W, GRID_W), GRID_W)
            vw = v_ref[pl.ds(pl.multiple_of(rs * GRID_W, GRID_W), nwin), :].astype(BF16)
            o = (jnp.dot(e_ref[qrow, :], vw, preferred_element_type=F32) + oc_ref[qrow, :]) * zi_ref[qrow, :]
            if second:
                o = jnp.where(lane < HEAD_DIM, y_ref[qrow, :].astype(F32), o)
            y_ref[qrow, :] = o.astype(y_ref.dtype)
            return carry

        lax.fori_loop(0, rows, values, 0, unroll=ATTN_UNROLL)


def _attn_lat_call(proj, ck, cv, bias, l, n_ctx_tok, n_batch, seq):
    col0 = (3 * A_WIDTH + B_WIDTH) // LANES
    ncol = C_WIDTH // LANES
    row0 = n_ctx_tok // seq
    past = ck.shape[2]
    return pl.pallas_call(
        _attn_lat_kernel,
        grid=(n_batch, ncol),
        in_specs=[
            pl.BlockSpec((seq, LANES), lambda b, hp: (row0 + b, col0 + hp)),
            pl.BlockSpec((seq, LANES), lambda b, hp: (row0 + b, col0 + ncol + hp)),
            pl.BlockSpec((seq, LANES), lambda b, hp: (row0 + b, col0 + 2 * ncol + hp)),
            pl.BlockSpec((1, 1, past, LANES), lambda b, hp: (b, l, 0, hp)),
            pl.BlockSpec((1, 1, past, LANES), lambda b, hp: (b, l, 0, hp)),
            pl.BlockSpec((1, 2, NA_ROWS, GRID_W, NA_ROWS * GRID_W), lambda b, hp: (l, hp, 0, 0, 0)),
        ],
        out_specs=pl.BlockSpec((seq, LANES), lambda b, hp: (b, hp)),
        out_shape=jax.ShapeDtypeStruct((n_batch * seq, C_WIDTH), BF16),
        scratch_shapes=[pltpu.VMEM((seq, NA_ROWS * GRID_W), F32), pltpu.VMEM((seq, NA_ROWS * GRID_W), BF16),
                        pltpu.VMEM((seq, LANES), F32), pltpu.VMEM((seq, LANES), F32)],
        compiler_params=_cparams(("arbitrary", "arbitrary")),
        name="attn_lat",
    )(proj, proj, proj, ck, cv, bias)


def _out_kernel(ycp_ref, yat_ref, x_ref, m_ref, g_ref, w_ref, xo_ref, h_ref, *, n_ctx_tok, dec_seq):
    d = D_MODEL
    half = A_WIDTH + B_WIDTH
    row = _mod_row(pl.program_id(0), x_ref.shape[0], n_ctx_tok, dec_seq)
    y = (jnp.dot(ycp_ref[...], w_ref[0, 0:half], preferred_element_type=F32)
         + jnp.dot(yat_ref[...], w_ref[0, half:2 * half], preferred_element_type=F32))
    x = x_ref[...] + m_ref[0, pl.ds(row, 1), 2 * d:3 * d] * y
    xo_ref[...] = x
    sh = m_ref[0, pl.ds(row, 1), 3 * d:4 * d]
    sc = m_ref[0, pl.ds(row, 1), 4 * d:5 * d]
    h_ref[...] = (_rms(x, g_ref[0]) * (1.0 + sc) + sh).astype(BF16)


def _out_call(ycp, yat, x, mods, g2, w_out, l, n_ctx_tok, dec_seq):
    t, d = x.shape
    half = ycp.shape[1]
    return pl.pallas_call(
        functools.partial(_out_kernel, n_ctx_tok=n_ctx_tok, dec_seq=dec_seq),
        grid=(t // TOK_BLK,),
        in_specs=[
            pl.BlockSpec((TOK_BLK, half), lambda i: (i, 0)),
            pl.BlockSpec((TOK_BLK, half), lambda i: (i, 0)),
            pl.BlockSpec((TOK_BLK, d), lambda i: (i, 0)),
            pl.BlockSpec((1, MOD_ROWS, N_MOD * d), lambda i: (l, 0, 0)),
            pl.BlockSpec((1, 1, d), lambda i: (l, 0, 0)),
            pl.BlockSpec((1, d, d), lambda i: (l, 0, 0)),
        ],
        out_specs=[pl.BlockSpec((TOK_BLK, d), lambda i: (i, 0)), pl.BlockSpec((TOK_BLK, d), lambda i: (i, 0))],
        out_shape=[jax.ShapeDtypeStruct((t, d), F32), jax.ShapeDtypeStruct((t, d), BF16)],
        compiler_params=_cparams(("arbitrary",)),
        name="out_proj",
    )(ycp, yat, x, mods, g2, w_out)


def _oem_sort_pairs(n):
    pairs = []

    def merge(lo, m, r):
        step = r * 2
        if step < m:
            merge(lo, m, step)
            merge(lo + r, m, step)
            pairs.extend((i, i + r) for i in range(lo + r, lo + m - r, step))
        else:
            pairs.append((lo, lo + r))

    def sort(lo, m):
        if m > 1:
            sort(lo, m // 2)
            sort(lo + m // 2, m // 2)
            merge(lo, m, 1)

    sort(0, n)
    return pairs


_SORT16 = _oem_sort_pairs(PEER_TOPK)
_CAND = [(a, b) for a in range(PEER_TOPK) for b in range(PEER_TOPK) if (a + 1) * (b + 1) <= PEER_TOPK]


def _cmp_exchange(vals, idxs, i, j):
    first = vals[i] >= vals[j]
    vals[i], vals[j] = jnp.maximum(vals[i], vals[j]), jnp.minimum(vals[i], vals[j])
    idxs[i], idxs[j] = jnp.where(first, idxs[i], idxs[j]), jnp.where(first, idxs[j], idxs[i])


def _top16(load):
    k = PEER_TOPK
    best = None
    for g in range(PEER_KEYS // k):
        vals = [load(g * k + i) for i in range(k)]
        idxs = [jnp.full(vals[0].shape, float(g * k + i), F32) for i in range(k)]
        for i, j in _SORT16:
            _cmp_exchange(vals, idxs, i, j)
        if best is not None:
            bv, bi = best
            for i in range(k):
                first = bv[i] >= vals[k - 1 - i]
                bi[i] = jnp.where(first, bi[i], idxs[k - 1 - i])
                bv[i] = jnp.maximum(bv[i], vals[k - 1 - i])
            stride = k // 2
            while stride >= 1:
                for i in range(k):
                    if i & stride == 0:
                        _cmp_exchange(bv, bi, i, i + stride)
                stride //= 2
            vals, idxs = bv, bi
        best = (vals, idxs)
    return best


def _gate_kernel(h_ref, w2_ref, o_ref, sc_ref, i1_ref, i2_ref, g_ref, gs_ref):
    nk = PEER_KEYS
    k = PEER_TOPK
    sc_ref[...] = lax.dot_general(w2_ref[0], h_ref[...], (((1,), (1,)), ((), ())), preferred_element_type=F32)

    def topk_group(sb):
        lanes = slice(sb * LANES, (sb + 1) * LANES)
        tops = [_top16(lambda key, p=p, lanes=lanes: sc_ref[pl.ds((p * nk + key) * PEER_HEADS, PEER_HEADS), lanes])
                for p in range(2)]
        (v1, x1), (v2, x2) = tops
        cand = [v1[a] + v2[b] for a, b in _CAND]
        slot_v, slot_1, slot_2 = [], [], []
        for it in range(k):
            live = [n for n, (a, b) in enumerate(_CAND) if (a + 1) * (b + 1) <= it + 1]
            mx = functools.reduce(jnp.maximum, [cand[n] for n in live])
            found = jnp.zeros(mx.shape, jnp.bool_)
            s1 = jnp.zeros(mx.shape, F32)
            s2 = jnp.zeros(mx.shape, F32)
            for n in live:
                a, b = _CAND[n]
                eq = cand[n] == mx
                sel = eq & ~found
                found = found | eq
                s1 = jnp.where(sel, x1[a], s1)
                s2 = jnp.where(sel, x2[b], s2)
                cand[n] = jnp.where(sel, -jnp.inf, cand[n])
            slot_v.append(mx)
            slot_1.append(s1)
            slot_2.append(s2)
        e = [jnp.exp(v - slot_v[0]) for v in slot_v]
        z = functools.reduce(jnp.add, e)
        gates = [x / z for x in e]
        i1_ref[lanes, :] = jnp.concatenate(slot_1, axis=0).T
        i2_ref[lanes, :] = jnp.concatenate(slot_2, axis=0).T
        g_ref[lanes, :] = jnp.concatenate(gates, axis=0).T
    pk = 2 * SUBLANES
    key_iota = lax.broadcasted_iota(jnp.int32, (nk // pk, pk, nk), 0) * pk + lax.broadcasted_iota(
        jnp.int32, (nk // pk, pk, nk), 1)
    key_iota = key_iota.astype(F32).astype(BF16)
    one = jnp.ones((), BF16)
    zero = jnp.zeros((), BF16)

    def row(ref, t):
        return jnp.broadcast_to(ref[pl.ds(t, 1), :], (pk, nk)).astype(BF16)[None]

    def per_token(t, carry):
        lt = jnp.where(key_iota == row(i1_ref, t), one, zero).reshape(nk, nk)
        rt = jnp.where(key_iota == row(i2_ref, t), row(g_ref, t), zero).reshape(nk, nk)
        gt = lax.dot_general(lt, rt, (((1,), (1,)), ((), ())), preferred_element_type=F32)
        gs_ref[pl.ds(pl.multiple_of(t * GATE_PITCH, SUBLANES), nk), :] = gt
        return carry

    for sb in range(GATE_BLK // LANES):
        topk_group(sb)
    lax.fori_loop(0, GATE_BLK, per_token, 0, unroll=GATE_UNROLL)
    for i1 in range(nk):
        o_ref[i1] = gs_ref[pl.ds(i1, GATE_BLK, stride=GATE_PITCH), :].astype(BF16)


def _gate_call(h2, w2t, l):
    t, d = h2.shape
    nrow = w2t.shape[1]
    return pl.pallas_call(
        _gate_kernel,
        grid=(t // GATE_BLK,),
        in_specs=[
            pl.BlockSpec((GATE_BLK, d), lambda i: (i, 0)),
            pl.BlockSpec((1, nrow, d), lambda i: (l, 0, 0)),
        ],
        out_specs=pl.BlockSpec((PEER_KEYS, GATE_BLK, PEER_KEYS), lambda i: (0, i, 0)),
        out_shape=jax.ShapeDtypeStruct((PEER_KEYS, t, PEER_KEYS), BF16),
        scratch_shapes=[
            pltpu.VMEM((nrow, GATE_BLK), F32),
            pltpu.VMEM((GATE_BLK, PEER_HEADS * PEER_TOPK), F32),
            pltpu.VMEM((GATE_BLK, PEER_HEADS * PEER_TOPK), F32),
            pltpu.VMEM((GATE_BLK, PEER_HEADS * PEER_TOPK), F32),
            pltpu.VMEM((GATE_BLK * GATE_PITCH, PEER_KEYS), F32),
        ],
        compiler_params=_cparams(("arbitrary",)),
        name="peer_gates",
    )(h2, w2t)


def _gelu(x):
    return 0.5 * x * (1.0 + jnp.tanh(0.7978845608028654 * (x + 0.044715 * (x * x * x))))


def _expert_kernel(h_ref, u_ref, v_ref, g_ref, x_ref, m_ref, o_ref, acc_ref, *, n_ctx_tok, dec_seq):
    d = D_MODEL
    j = pl.program_id(1)

    @pl.when(j == 0)
    def _():
        acc_ref[...] = jnp.zeros_like(acc_ref)

    a = lax.dot_general(h_ref[...], u_ref[0].astype(BF16), (((1,), (1,)), ((), ())), preferred_element_type=F32)
    hid = jnp.concatenate(
        [(_gelu(a[:, q * PEER_KEYS:(q + 1) * PEER_KEYS]) * g_ref[q].astype(F32)).astype(BF16)
         for q in range(EXP_PLANES)], axis=1)
    acc_ref[...] += jnp.dot(hid, v_ref[0].astype(BF16), preferred_element_type=F32)

    @pl.when(j == pl.num_programs(1) - 1)
    def _():
        row = _mod_row(pl.program_id(0), x_ref.shape[0], n_ctx_tok, dec_seq)
        o_ref[...] = x_ref[...] + m_ref[0, pl.ds(row, 1), 5 * d:6 * d] * acc_ref[...]


def _expert_call(h2, u, v, gates, x, mods, l, n_ctx_tok, dec_seq):
    t, d = x.shape
    ne = u.shape[1]
    return pl.pallas_call(
        functools.partial(_expert_kernel, n_ctx_tok=n_ctx_tok, dec_seq=dec_seq),
        grid=(t // EXP_TOK, ne // EXP_BLK),
        in_specs=[
            pl.BlockSpec((EXP_TOK, d), lambda i, j: (i, 0)),
            pl.BlockSpec((1, EXP_BLK, d), lambda i, j: (l, j, 0)),
            pl.BlockSpec((1, EXP_BLK, d), lambda i, j: (l, j, 0)),
            pl.BlockSpec((EXP_PLANES, EXP_TOK, PEER_KEYS), lambda i, j: (j, i, 0)),
            pl.BlockSpec((EXP_TOK, d), lambda i, j: (i, 0)),
            pl.BlockSpec((1, MOD_ROWS, N_MOD * d), lambda i, j: (l, 0, 0)),
        ],
        out_specs=pl.BlockSpec((EXP_TOK, d), lambda i, j: (i, 0)),
        out_shape=jax.ShapeDtypeStruct((t, d), F32),
        scratch_shapes=[pltpu.VMEM((EXP_TOK, d), F32)],
        compiler_params=_cparams(("arbitrary", "arbitrary")),
        name="peer_experts",
    )(h2, u, v, gates, x, mods)


def _final_kernel(x_ref, g_ref, o_ref):
    o_ref[...] = _rms(x_ref[...], g_ref[...])


def _final_call(x, g, row0, n_rows):
    d = x.shape[1]
    blk0 = row0 // TOK_BLK
    return pl.pallas_call(
        _final_kernel,
        grid=(n_rows // TOK_BLK,),
        in_specs=[pl.BlockSpec((TOK_BLK, d), lambda i: (blk0 + i, 0)), pl.BlockSpec((1, d), lambda i: (0, 0))],
        out_specs=pl.BlockSpec((TOK_BLK, d), lambda i: (i, 0)),
        out_shape=jax.ShapeDtypeStruct((n_rows, d), F32),
        compiler_params=_cparams(("arbitrary",)),
        name="final_norm",
    )(x, g.reshape(1, d))


def kernel(x_prompt, x_sample, c, cache_k, cache_v, c_ctx, norm1_g, norm2_g, w_ada, b_ada, w_in, conv_w,
           pool_w, pool_scale, rpb, w_out, peer_wq, peer_subkeys, peer_u, peer_v, final_g):
    n_b, seq_c, d = x_prompt.shape
    n_db, seq_l, _ = x_sample.shape
    depth = w_ada.shape[0]
    n_ctx_tok = n_b * seq_c
    assert d == D_MODEL and seq_l % GRID_W == 0 and seq_l // GRID_W >= NA_ROWS
    assert n_ctx_tok % MIX_BLK == 0 and seq_l % MIX_BLK == 0 and MIX_BLK % seq_c == 0
    assert 1 + n_db <= MOD_ROWS and (n_ctx_tok + n_db * seq_l) % EXP_TOK == 0

    x = jnp.concatenate([x_prompt.reshape(n_ctx_tok, d), x_sample.reshape(n_db * seq_l, d)], axis=0)
    cvec = jnp.concatenate([c_ctx[None, :], c, jnp.zeros((MOD_ROWS - 1 - n_db, d), F32)], axis=0)
    mods = _mods_call(cvec, w_ada, b_ada)
    w2t = _w2_call(peer_wq, peer_subkeys)
    w2t = w2t.transpose(0, 1, 3, 2, 4).reshape(depth, 2 * PEER_KEYS * PEER_HEADS, d)
    bias = _bias_call(rpb)
    past = cache_k.shape[3]
    ck = cache_k.transpose(0, 1, 3, 2, 4).reshape(n_db, depth, past, C_WIDTH)
    cv = cache_v.transpose(0, 1, 3, 2, 4).reshape(n_db, depth, past, C_WIDTH)
    w_in_b = w_in.astype(BF16)
    w_out_b = w_out.astype(BF16)
    eye = jnp.eye(B_WIDTH // POOL_GROUP_C, dtype=F32)
    pool_bd = jnp.einsum('gh,lgcd->lgchd', eye, pool_w).reshape(depth, B_WIDTH, B_WIDTH).astype(BF16)
    g1 = norm1_g.reshape(depth, 1, d)
    g2 = norm2_g.reshape(depth, 1, d)
    ps = pool_scale.reshape(depth, 1, B_WIDTH)

    k_new = jnp.zeros((n_b, depth, N_HEADS_ATTN, seq_c, HEAD_DIM), F32)
    v_new = jnp.zeros((n_b, depth, N_HEADS_ATTN, seq_c, HEAD_DIM), F32)
    for l in range(depth):
        proj = _in_call(x, mods, g1, w_in_b, l, n_ctx_tok, seq_l)
        ycp = _mix_call(proj, conv_w, pool_bd, ps, l, n_ctx_tok, seq_c, seq_l)
        y_c, k_new, v_new = _attn_ctx_call(proj, k_new, v_new, l, n_b, seq_c)
        y_l = _attn_lat_call(proj, ck, cv, bias, l, n_ctx_tok, n_db, seq_l)
        yat = jnp.concatenate([y_c, y_l], axis=0)
        x, h2 = _out_call(ycp, yat, x, mods, g2, w_out_b, l, n_ctx_tok, seq_l)
        gates = _gate_call(h2, w2t, l)
        x = _expert_call(h2, peer_u, peer_v, gates, x, mods, l, n_ctx_tok, seq_l)
    y_prompt = _final_call(x, final_g, 0, n_ctx_tok).reshape(n_b, seq_c, d)
    y_sample = _final_call(x, final_g, n_ctx_tok, n_db * seq_l).reshape(n_db, seq_l, d)
    return (y_prompt, y_sample, k_new, v_new)
```

```python
import functools

import numpy as np
import jax
import jax.numpy as jnp
from jax import lax
from jax.experimental import pallas as pl
from jax.experimental.pallas import tpu as pltpu

F32 = jnp.float32
BF16 = jnp.bfloat16

D_MODEL = 1024
DEPTH = 4
GRID_W = 64
A_WIDTH = D_MODEL // 4
B_WIDTH = D_MODEL // 4
C_WIDTH = D_MODEL // 2
N_HEADS_ATTN = 8
HEAD_DIM = C_WIDTH // N_HEADS_ATTN
POOL_GROUP_C = B_WIDTH // 4
NA_ROWS = 8
NA_COLS = 16
PEER_HEADS = 8
PEER_KEYS = 128
PEER_EXPERTS = PEER_KEYS * PEER_KEYS
PEER_TOPK = 16
PEER_HALF = 128
IN_COLS = 3 * A_WIDTH + B_WIDTH + 3 * C_WIDTH
N_MOD = 6
EPS = 1e-6
NEG_INF = -1e30
ATTN_SCALE = HEAD_DIM ** -0.5

LANES = 128
SUBLANES = 8
MOD_ROWS = 8
VMEM_LIMIT = 56 * 1024 * 1024

TOK_BLK = 512
MIX_BLK = 2048
GATE_BLK = 256
GATE_PITCH = PEER_KEYS + SUBLANES
GATE_UNROLL = 64
ATTN_CTX_COLS = 256
ATTN_UNROLL = 16
EXP_TOK = 1024
EXP_BLK = 1024
EXP_PLANES = EXP_BLK // PEER_KEYS


def _cparams(sem):
    return pltpu.CompilerParams(dimension_semantics=sem, vmem_limit_bytes=VMEM_LIMIT)


def _rms(x, g):
    return x * lax.rsqrt(jnp.mean(x * x, axis=-1, keepdims=True) + EPS) * g


def _mod_row(i, blk, n_ctx_tok, dec_seq):
    start = i * blk
    return jnp.where(start < n_ctx_tok, 0, 1 + jnp.maximum(start - n_ctx_tok, 0) // dec_seq)


def _mods_kernel(c_ref, w_ref, b_ref, o_ref):
    c = c_ref[...]
    s = (c * jax.nn.sigmoid(c)).astype(BF16)
    o_ref[0] = jnp.dot(s, w_ref[0].astype(BF16), preferred_element_type=F32) + b_ref[0]


def _mods_call(cvec, w_ada, b_ada):
    depth, d, n = w_ada.shape
    tn = 1536
    return pl.pallas_call(
        _mods_kernel,
        grid=(depth, n // tn),
        in_specs=[
            pl.BlockSpec((MOD_ROWS, d), lambda l, j: (0, 0)),
            pl.BlockSpec((1, d, tn), lambda l, j: (l, 0, j)),
            pl.BlockSpec((1, 1, tn), lambda l, j: (l, 0, j)),
        ],
        out_specs=pl.BlockSpec((1, MOD_ROWS, tn), lambda l, j: (l, 0, j)),
        out_shape=jax.ShapeDtypeStruct((depth, MOD_ROWS, n), F32),
        compiler_params=_cparams(("arbitrary", "arbitrary")),
        name="mods",
    )(cvec, w_ada, b_ada.reshape(depth, 1, n))


def _w2_kernel(k_ref, wq_ref, o_ref):
    for h in range(PEER_HEADS):
        for p in range(2):
            c0 = (2 * h + p) * PEER_HALF
            o_ref[0, p, h] = lax.dot_general(
                k_ref[0, h, p], wq_ref[0, :, c0:c0 + PEER_HALF], (((1,), (1,)), ((), ())),
                precision=lax.Precision.HIGHEST, preferred_element_type=F32).astype(BF16)


def _w2_call(peer_wq, peer_subkeys):
    depth, d, nq = peer_wq.shape
    return pl.pallas_call(
        _w2_kernel,
        grid=(depth,),
        in_specs=[
            pl.BlockSpec((1, PEER_HEADS, 2, PEER_KEYS, PEER_HALF), lambda l: (l, 0, 0, 0, 0)),
            pl.BlockSpec((1, d, nq), lambda l: (l, 0, 0)),
        ],
        out_specs=pl.BlockSpec((1, 2, PEER_HEADS, PEER_KEYS, d), lambda l: (l, 0, 0, 0, 0)),
        out_shape=jax.ShapeDtypeStruct((depth, 2, PEER_HEADS, PEER_KEYS, d), BF16),
        compiler_params=_cparams(("arbitrary",)),
        name="peer_w2",
    )(peer_subkeys, peer_wq)


def _bias_kernel(rpb_ref, o_ref):
    l = pl.program_id(0)
    h = pl.program_id(1)
    n_dr = 2 * NA_ROWS - 1
    n_dc = 2 * NA_COLS - 1
    base = (l * N_HEADS_ATTN + h) * (n_dr * n_dc)
    c = lax.broadcasted_iota(jnp.int32, (GRID_W, LANES), 0)
    lane = lax.broadcasted_iota(jnp.int32, (GRID_W, LANES), 1)
    kc = lane & (GRID_W - 1)
    second = lane >= GRID_W
    diff = kc - c + (NA_COLS - 1)
    cs = jnp.clip(c - NA_COLS // 2, 0, GRID_W - NA_COLS)
    valid = (kc >= cs) & (kc < cs + NA_COLS)
    tiles = []
    for dr in range(n_dr - 1):
        t = jnp.zeros((GRID_W, LANES), F32)
        for dc in range(n_dc):
            hit = diff == dc
            t = jnp.where(hit & ~second, rpb_ref[base + dr * n_dc + dc], t)
            t = jnp.where(hit & second, rpb_ref[base + (dr + 1) * n_dc + dc], t)
        tiles.append(jnp.where(valid, t, NEG_INF))
    for v in range(NA_ROWS):
        for m in range(NA_ROWS // 2):
            o_ref[0, 0, v, :, m * LANES:(m + 1) * LANES] = tiles[2 * m - v + NA_ROWS - 1]


def _bias_call(rpb):
    depth = rpb.shape[0]
    return pl.pallas_call(
        _bias_kernel,
        grid=(depth, N_HEADS_ATTN),
        in_specs=[pl.BlockSpec(memory_space=pltpu.SMEM)],
        out_specs=pl.BlockSpec((1, 1, NA_ROWS, GRID_W, NA_ROWS * GRID_W), lambda l, h: (l, h, 0, 0, 0)),
        out_shape=jax.ShapeDtypeStruct((depth, N_HEADS_ATTN, NA_ROWS, GRID_W, NA_ROWS * GRID_W), F32),
        compiler_params=_cparams(("arbitrary", "arbitrary")),
        name="na_bias",
    )(rpb.reshape(-1))


def _in_kernel(x_ref, m_ref, g_ref, w_ref, o_ref, *, n_ctx_tok, dec_seq):
    d = D_MODEL
    row = _mod_row(pl.program_id(0), x_ref.shape[0], n_ctx_tok, dec_seq)
    sh = m_ref[0, pl.ds(row, 1), 0:d]
    sc = m_ref[0, pl.ds(row, 1), d:2 * d]
    h = _rms(x_ref[...], g_ref[0]) * (1.0 + sc) + sh
    o_ref[...] = jnp.dot(h.astype(BF16), w_ref[0], preferred_element_type=F32)


def _in_call(x, mods, g1, w_in, l, n_ctx_tok, dec_seq):
    t, d = x.shape
    n = w_in.shape[-1]
    return pl.pallas_call(
        functools.partial(_in_kernel, n_ctx_tok=n_ctx_tok, dec_seq=dec_seq),
        grid=(t // TOK_BLK,),
        in_specs=[
            pl.BlockSpec((TOK_BLK, d), lambda i: (i, 0)),
            pl.BlockSpec((1, MOD_ROWS, N_MOD * d), lambda i: (l, 0, 0)),
            pl.BlockSpec((1, 1, d), lambda i: (l, 0, 0)),
            pl.BlockSpec((1, d, n), lambda i: (l, 0, 0)),
        ],
        out_specs=pl.BlockSpec((TOK_BLK, n), lambda i: (i, 0)),
        out_shape=jax.ShapeDtypeStruct((t, n), F32),
        compiler_params=_cparams(("arbitrary",)),
        name="in_proj",
    )(x, mods, g1, w_in)


def _mix_kernel(p_ref, cw_ref, bd_ref, ps_ref, o_ref, *, n_ctx_blk, seq_c, seq_l):
    blk = p_ref.shape[0]
    lseq = jnp.where(pl.program_id(0) < n_ctx_blk, seq_c, seq_l)
    rows = lax.broadcasted_iota(jnp.int32, (blk, A_WIDTH), 0)
    lane = lax.broadcasted_iota(jnp.int32, (blk, A_WIDTH), 1)
    tpos = rows & (lseq - 1)

    def shift_dn(x, m):
        return jnp.where(tpos >= m, pltpu.roll(x, m, 0), 0.0)

    def shift_up(x, m):
        return jnp.where(tpos + m < lseq, pltpu.roll(x, blk - m, 0), 0.0)

    a_in = p_ref[:, 0:A_WIDTH]
    a_b = p_ref[:, A_WIDTH:2 * A_WIDTH]
    a_c = p_ref[:, 2 * A_WIDTH:3 * A_WIDTH]
    u = a_c * a_in
    cw = cw_ref[0]
    y = cw[0:1] * shift_dn(u, 1) + cw[1:2] * u + cw[2:3] * shift_up(u, 1)
    o_ref[:, 0:A_WIDTH] = (a_b * y).astype(o_ref.dtype)

    p = p_ref[:, 3 * A_WIDTH:3 * A_WIDTH + B_WIDTH]
    f1 = p
    f2 = f1 + shift_up(f1, 1)
    f4 = f2 + shift_up(f2, 2)
    f8 = f4 + shift_up(f4, 4)
    b1 = shift_dn(p, 1)
    b2 = b1 + shift_dn(b1, 1)
    b4 = b2 + shift_dn(b2, 2)
    b8 = b4 + shift_dn(b4, 4)
    grp = jnp.right_shift(lane, POOL_GROUP_C.bit_length() - 1)
    wsum = jnp.where(grp == 0, b1 + f1, jnp.where(grp == 1, b2 + f2, jnp.where(grp == 2, b4 + f4, b8 + f8)))
    half = jnp.left_shift(1, grp)
    lo = jnp.maximum(tpos - half, 0)
    hi = jnp.minimum(tpos + half - 1, lseq - 1)
    cnt = (hi - lo + 1).astype(F32)
    dlt = wsum / cnt - p
    o_ref[:, A_WIDTH:A_WIDTH + B_WIDTH] = (
        jnp.dot(dlt.astype(BF16), bd_ref[0], preferred_element_type=F32) * ps_ref[0]).astype(o_ref.dtype)


def _mix_call(proj, conv_w, pool_bd, pool_scale, l, n_ctx_tok, seq_c, seq_l):
    t = proj.shape[0]
    return pl.pallas_call(
        functools.partial(_mix_kernel, n_ctx_blk=n_ctx_tok // MIX_BLK, seq_c=seq_c, seq_l=seq_l),
        grid=(t // MIX_BLK,),
        in_specs=[
            pl.BlockSpec((MIX_BLK, 3 * A_WIDTH + B_WIDTH), lambda i: (i, 0)),
            pl.BlockSpec((1, 3, A_WIDTH), lambda i: (l, 0, 0)),
            pl.BlockSpec((1, B_WIDTH, B_WIDTH), lambda i: (l, 0, 0)),
            pl.BlockSpec((1, 1, B_WIDTH), lambda i: (l, 0, 0)),
        ],
        out_specs=pl.BlockSpec((MIX_BLK, A_WIDTH + B_WIDTH), lambda i: (i, 0)),
        out_shape=jax.ShapeDtypeStruct((t, A_WIDTH + B_WIDTH), BF16),
        compiler_params=_cparams(("arbitrary",)),
        name="conv_pool",
    )(proj, conv_w, pool_bd, pool_scale)


def _attn_ctx_kernel(q_ref, k_ref, v_ref, kin_ref, vin_ref, y_ref, ko_ref, vo_ref):
    del kin_ref, vin_ref
    q = q_ref[...]
    k = k_ref[...]
    v = v_ref[...]
    head = jnp.right_shift(lax.broadcasted_iota(jnp.int32, q.shape, 1), HEAD_DIM.bit_length() - 1)
    kb = k.astype(BF16)
    vb = v.astype(BF16)
    y = jnp.zeros(q.shape, F32)
    for h in range(q.shape[1] // HEAD_DIM):
        qh = jnp.where(head == h, q, 0.0).astype(BF16)
        s = lax.dot_general(qh, kb, (((1,), (1,)), ((), ())), preferred_element_type=F32) * ATTN_SCALE
        e = jnp.exp(s - jnp.max(s, axis=-1, keepdims=True))
        pr = e / jnp.sum(e, axis=-1, keepdims=True)
        y = jnp.where(head == h, jnp.dot(pr.astype(BF16), vb, preferred_element_type=F32), y)
        ko_ref[0, 0, h] = k[:, h * HEAD_DIM:(h + 1) * HEAD_DIM]
        vo_ref[0, 0, h] = v[:, h * HEAD_DIM:(h + 1) * HEAD_DIM]
    y_ref[...] = y.astype(y_ref.dtype)


def _attn_ctx_call(proj, k_new, v_new, l, n_batch, seq):
    width = ATTN_CTX_COLS
    col0 = (3 * A_WIDTH + B_WIDTH) // width
    ncol = C_WIDTH // width
    kv_shape = jax.ShapeDtypeStruct(k_new.shape, F32)
    kv_spec = pl.BlockSpec((1, 1, width // HEAD_DIM, seq, HEAD_DIM), lambda b, hp: (b, l, hp, 0, 0))
    return pl.pallas_call(
        _attn_ctx_kernel,
        grid=(n_batch, ncol),
        in_specs=[
            pl.BlockSpec((seq, width), lambda b, hp: (b, col0 + hp)),
            pl.BlockSpec((seq, width), lambda b, hp: (b, col0 + ncol + hp)),
            pl.BlockSpec((seq, width), lambda b, hp: (b, col0 + 2 * ncol + hp)),
            pl.BlockSpec(memory_space=pl.ANY),
            pl.BlockSpec(memory_space=pl.ANY),
        ],
        out_specs=[pl.BlockSpec((seq, width), lambda b, hp: (b, hp)), kv_spec, kv_spec],
        out_shape=[jax.ShapeDtypeStruct((n_batch * seq, C_WIDTH), BF16), kv_shape, kv_shape],
        input_output_aliases={3: 1, 4: 2},
        compiler_params=_cparams(("arbitrary", "arbitrary")),
        name="attn_ctx",
    )(proj, proj, proj, k_new, v_new)


def _attn_lat_kernel(q_ref, k_ref, v_ref, ck_ref, cv_ref, bv_ref, y_ref, s_ref, e_ref, oc_ref, zi_ref):
    n = q_ref.shape[0]
    rows = n // GRID_W
    nrow = min(NA_ROWS, rows)
    nwin = nrow * GRID_W
    dn = (((1,), (1,)), ((), ()))
    ck = ck_ref[0, 0].astype(BF16)
    cv = cv_ref[0, 0].astype(BF16)
    lane_all = lax.broadcasted_iota(jnp.int32, (n, LANES), 1)
    lane = lax.broadcasted_iota(jnp.int32, (GRID_W, LANES), 1)

    for h in range(2):
        second = h == 1

        def scores(r, carry):
            rs = jnp.clip(r - nrow // 2, 0, rows - nrow)
            qrow = pl.ds(pl.multiple_of(r * GRID_W, GRID_W), GRID_W)
            qh = jnp.where((lane >= HEAD_DIM) == second, q_ref[qrow, :], 0.0).astype(BF16)
            kw = k_ref[pl.ds(pl.multiple_of(rs * GRID_W, GRID_W), nwin), :].astype(BF16)
            s_ref[qrow, :] = (lax.dot_general(qh, kw, dn, preferred_element_type=F32) * ATTN_SCALE
                              + bv_ref[0, h, r - rs])
            return carry

        lax.fori_loop(0, rows, scores, 0, unroll=ATTN_UNROLL)

        qh_all = jnp.where((lane_all >= HEAD_DIM) == second, q_ref[...], 0.0).astype(BF16)
        s_c = lax.dot_general(qh_all, ck, dn, preferred_element_type=F32) * ATTN_SCALE
        s_w = s_ref[...]
        mx = jnp.maximum(jnp.max(s_w, axis=-1, keepdims=True), jnp.max(s_c, axis=-1, keepdims=True))
        e_w = jnp.exp(s_w - mx)
        e_c = jnp.exp(s_c - mx)
        z = jnp.sum(e_w, axis=-1, keepdims=True) + jnp.sum(e_c, axis=-1, keepdims=True)
        e_ref[...] = e_w.astype(BF16)
        oc_ref[...] = jnp.dot(e_c.astype(BF16), cv, preferred_element_type=F32)
        zi_ref[...] = jnp.broadcast_to(1.0 / z, (n, LANES))

        def values(r, carry):
            rs = jnp.clip(r - nrow // 2, 0, rows - nrow)
            qrow = pl.ds(pl.multiple_of(r * GRID_W, GRID_W), GRID_W)
            vw = v_ref[pl.ds(pl.multiple_of(rs * GRID_W, GRID_W), nwin), :].astype(BF16)
            o = (jnp.dot(e_ref[qrow, :], vw, preferred_element_type=F32) + oc_ref[qrow, :]) * zi_ref[qrow, :]
            if second:
                o = jnp.where(lane < HEAD_DIM, y_ref[qrow, :].astype(F32), o)
            y_ref[qrow, :] = o.astype(y_ref.dtype)
            return carry

        lax.fori_loop(0, rows, values, 0, unroll=ATTN_UNROLL)


def _attn_lat_call(proj, ck, cv, bias, l, n_ctx_tok, n_batch, seq):
    col0 = (3 * A_WIDTH + B_WIDTH) // LANES
    ncol = C_WIDTH // LANES
    row0 = n_ctx_tok // seq
    past = ck.shape[2]
    return pl.pallas_call(
        _attn_lat_kernel,
        grid=(n_batch, ncol),
        in_specs=[
            pl.BlockSpec((seq, LANES), lambda b, hp: (row0 + b, col0 + hp)),
            pl.BlockSpec((seq, LANES), lambda b, hp: (row0 + b, col0 + ncol + hp)),
            pl.BlockSpec((seq, LANES), lambda b, hp: (row0 + b, col0 + 2 * ncol + hp)),
            pl.BlockSpec((1, 1, past, LANES), lambda b, hp: (b, l, 0, hp)),
            pl.BlockSpec((1, 1, past, LANES), lambda b, hp: (b, l, 0, hp)),
            pl.BlockSpec((1, 2, NA_ROWS, GRID_W, NA_ROWS * GRID_W), lambda b, hp: (l, hp, 0, 0, 0)),
        ],
        out_specs=pl.BlockSpec((seq, LANES), lambda b, hp: (b, hp)),
        out_shape=jax.ShapeDtypeStruct((n_batch * seq, C_WIDTH), BF16),
        scratch_shapes=[pltpu.VMEM((seq, NA_ROWS * GRID_W), F32), pltpu.VMEM((seq, NA_ROWS * GRID_W), BF16),
                        pltpu.VMEM((seq, LANES), F32), pltpu.VMEM((seq, LANES), F32)],
        compiler_params=_cparams(("arbitrary", "arbitrary")),
        name="attn_lat",
    )(proj, proj, proj, ck, cv, bias)


def _out_kernel(ycp_ref, yat_ref, x_ref, m_ref, g_ref, w_ref, xo_ref, h_ref, *, n_ctx_tok, dec_seq):
    d = D_MODEL
    half = A_WIDTH + B_WIDTH
    row = _mod_row(pl.program_id(0), x_ref.shape[0], n_ctx_tok, dec_seq)
    y = (jnp.dot(ycp_ref[...], w_ref[0, 0:half], preferred_element_type=F32)
         + jnp.dot(yat_ref[...], w_ref[0, half:2 * half], preferred_element_type=F32))
    x = x_ref[...] + m_ref[0, pl.ds(row, 1), 2 * d:3 * d] * y
    xo_ref[...] = x
    sh = m_ref[0, pl.ds(row, 1), 3 * d:4 * d]
    sc = m_ref[0, pl.ds(row, 1), 4 * d:5 * d]
    h_ref[...] = (_rms(x, g_ref[0]) * (1.0 + sc) + sh).astype(BF16)


def _out_call(ycp, yat, x, mods, g2, w_out, l, n_ctx_tok, dec_seq):
    t, d = x.shape
    half = ycp.shape[1]
    return pl.pallas_call(
        functools.partial(_out_kernel, n_ctx_tok=n_ctx_tok, dec_seq=dec_seq),
        grid=(t // TOK_BLK,),
        in_specs=[
            pl.BlockSpec((TOK_BLK, half), lambda i: (i, 0)),
            pl.BlockSpec((TOK_BLK, half), lambda i: (i, 0)),
            pl.BlockSpec((TOK_BLK, d), lambda i: (i, 0)),
            pl.BlockSpec((1, MOD_ROWS, N_MOD * d), lambda i: (l, 0, 0)),
            pl.BlockSpec((1, 1, d), lambda i: (l, 0, 0)),
            pl.BlockSpec((1, d, d), lambda i: (l, 0, 0)),
        ],
        out_specs=[pl.BlockSpec((TOK_BLK, d), lambda i: (i, 0)), pl.BlockSpec((TOK_BLK, d), lambda i: (i, 0))],
        out_shape=[jax.ShapeDtypeStruct((t, d), F32), jax.ShapeDtypeStruct((t, d), BF16)],
        compiler_params=_cparams(("arbitrary",)),
        name="out_proj",
    )(ycp, yat, x, mods, g2, w_out)


def _oem_sort_pairs(n):
    pairs = []

    def merge(lo, m, r):
        step = r * 2
        if step < m:
            merge(lo, m, step)
            merge(lo + r, m, step)
            pairs.extend((i, i + r) for i in range(lo + r, lo + m - r, step))
        else:
            pairs.append((lo, lo + r))

    def sort(lo, m):
        if m > 1:
            sort(lo, m // 2)
            sort(lo + m // 2, m // 2)
            merge(lo, m, 1)

    sort(0, n)
    return pairs


_SORT16 = _oem_sort_pairs(PEER_TOPK)
_CAND = [(a, b) for a in range(PEER_TOPK) for b in range(PEER_TOPK) if (a + 1) * (b + 1) <= PEER_TOPK]


def _cmp_exchange(vals, idxs, i, j):
    first = vals[i] >= vals[j]
    vals[i], vals[j] = jnp.maximum(vals[i], vals[j]), jnp.minimum(vals[i], vals[j])
    idxs[i], idxs[j] = jnp.where(first, idxs[i], idxs[j]), jnp.where(first, idxs[j], idxs[i])


def _top16(load):
    k = PEER_TOPK
    best = None
    for g in range(PEER_KEYS // k):
        vals = [load(g * k + i) for i in range(k)]
        idxs = [jnp.full(vals[0].shape, float(g * k + i), F32) for i in range(k)]
        for i, j in _SORT16:
            _cmp_exchange(vals, idxs, i, j)
        if best is not None:
            bv, bi = best
            for i in range(k):
                first = bv[i] >= vals[k - 1 - i]
                bi[i] = jnp.where(first, bi[i], idxs[k - 1 - i])
                bv[i] = jnp.maximum(bv[i], vals[k - 1 - i])
            stride = k // 2
            while stride >= 1:
                for i in range(k):
                    if i & stride == 0:
                        _cmp_exchange(bv, bi, i, i + stride)
                stride //= 2
            vals, idxs = bv, bi
        best = (vals, idxs)
    return best


def _gate_kernel(h_ref, w2_ref, o_ref, sc_ref, i1_ref, i2_ref, g_ref, gs_ref):
    nk = PEER_KEYS
    k = PEER_TOPK
    sc_ref[...] = lax.dot_general(w2_ref[0], h_ref[...], (((1,), (1,)), ((), ())), preferred_element_type=F32)

    def topk_group(sb):
        lanes = slice(sb * LANES, (sb + 1) * LANES)
        tops = [_top16(lambda key, p=p, lanes=lanes: sc_ref[pl.ds((p * nk + key) * PEER_HEADS, PEER_HEADS), lanes])
                for p in range(2)]
        (v1, x1), (v2, x2) = tops
        cand = [v1[a] + v2[b] for a, b in _CAND]
        slot_v, slot_1, slot_2 = [], [], []
        for it in range(k):
            live = [n for n, (a, b) in enumerate(_CAND) if (a + 1) * (b + 1) <= it + 1]
            mx = functools.reduce(jnp.maximum, [cand[n] for n in live])
            found = jnp.zeros(mx.shape, jnp.bool_)
            s1 = jnp.zeros(mx.shape, F32)
            s2 = jnp.zeros(mx.shape, F32)
            for n in live:
                a, b = _CAND[n]
                eq = cand[n] == mx
                sel = eq & ~found
                found = found | eq
                s1 = jnp.where(sel, x1[a], s1)
                s2 = jnp.where(sel, x2[b], s2)
                cand[n] = jnp.where(sel, -jnp.inf, cand[n])
            slot_v.append(mx)
            slot_1.append(s1)
            slot_2.append(s2)
        e = [jnp.exp(v - slot_v[0]) for v in slot_v]
        z = functools.reduce(jnp.add, e)
        gates = [x / z for x in e]
        i1_ref[lanes, :] = jnp.concatenate(slot_1, axis=0).T
        i2_ref[lanes, :] = jnp.concatenate(slot_2, axis=0).T
        g_ref[lanes, :] = jnp.concatenate(gates, axis=0).T
    pk = 2 * SUBLANES
    key_iota = lax.broadcasted_iota(jnp.int32, (nk // pk, pk, nk), 0) * pk + lax.broadcasted_iota(
        jnp.int32, (nk // pk, pk, nk), 1)
    key_iota = key_iota.astype(F32).astype(BF16)
    one = jnp.ones((), BF16)
    zero = jnp.zeros((), BF16)

    def row(ref, t):
        return jnp.broadcast_to(ref[pl.ds(t, 1), :], (pk, nk)).astype(BF16)[None]

    def per_token(t, carry):
        lt = jnp.where(key_iota == row(i1_ref, t), one, zero).reshape(nk, nk)
        rt = jnp.where(key_iota == row(i2_ref, t), row(g_ref, t), zero).reshape(nk, nk)
        gt = lax.dot_general(lt, rt, (((1,), (1,)), ((), ())), preferred_element_type=F32)
        gs_ref[pl.ds(pl.multiple_of(t * GATE_PITCH, SUBLANES), nk), :] = gt
        return carry

    for sb in range(GATE_BLK // LANES):
        topk_group(sb)
    lax.fori_loop(0, GATE_BLK, per_token, 0, unroll=GATE_UNROLL)
    for i1 in range(nk):
        o_ref[i1] = gs_ref[pl.ds(i1, GATE_BLK, stride=GATE_PITCH), :].astype(BF16)


def _gate_call(h2, w2t, l):
    t, d = h2.shape
    nrow = w2t.shape[1]
    return pl.pallas_call(
        _gate_kernel,
        grid=(t // GATE_BLK,),
        in_specs=[
            pl.BlockSpec((GATE_BLK, d), lambda i: (i, 0)),
            pl.BlockSpec((1, nrow, d), lambda i: (l, 0, 0)),
        ],
        out_specs=pl.BlockSpec((PEER_KEYS, GATE_BLK, PEER_KEYS), lambda i: (0, i, 0)),
        out_shape=jax.ShapeDtypeStruct((PEER_KEYS, t, PEER_KEYS), BF16),
        scratch_shapes=[
            pltpu.VMEM((nrow, GATE_BLK), F32),
            pltpu.VMEM((GATE_BLK, PEER_HEADS * PEER_TOPK), F32),
            pltpu.VMEM((GATE_BLK, PEER_HEADS * PEER_TOPK), F32),
            pltpu.VMEM((GATE_BLK, PEER_HEADS * PEER_TOPK), F32),
            pltpu.VMEM((GATE_BLK * GATE_PITCH, PEER_KEYS), F32),
        ],
        compiler_params=_cparams(("arbitrary",)),
        name="peer_gates",
    )(h2, w2t)


def _gelu(x):
    return 0.5 * x * (1.0 + jnp.tanh(0.7978845608028654 * (x + 0.044715 * (x * x * x))))


def _expert_kernel(h_ref, u_ref, v_ref, g_ref, x_ref, m_ref, o_ref, acc_ref, *, n_ctx_tok, dec_seq):
    d = D_MODEL
    j = pl.program_id(1)

    @pl.when(j == 0)
    def _():
        acc_ref[...] = jnp.zeros_like(acc_ref)

    a = lax.dot_general(h_ref[...], u_ref[0].astype(BF16), (((1,), (1,)), ((), ())), preferred_element_type=F32)
    hid = jnp.concatenate(
        [(_gelu(a[:, q * PEER_KEYS:(q + 1) * PEER_KEYS]) * g_ref[q].astype(F32)).astype(BF16)
         for q in range(EXP_PLANES)], axis=1)
    acc_ref[...] += jnp.dot(hid, v_ref[0].astype(BF16), preferred_element_type=F32)

    @pl.when(j == pl.num_programs(1) - 1)
    def _():
        row = _mod_row(pl.program_id(0), x_ref.shape[0], n_ctx_tok, dec_seq)
        o_ref[...] = x_ref[...] + m_ref[0, pl.ds(row, 1), 5 * d:6 * d] * acc_ref[...]


def _expert_call(h2, u, v, gates, x, mods, l, n_ctx_tok, dec_seq):
    t, d = x.shape
    ne = u.shape[1]
    return pl.pallas_call(
        functools.partial(_expert_kernel, n_ctx_tok=n_ctx_tok, dec_seq=dec_seq),
        grid=(t // EXP_TOK, ne // EXP_BLK),
        in_specs=[
            pl.BlockSpec((EXP_TOK, d), lambda i, j: (i, 0)),
            pl.BlockSpec((1, EXP_BLK, d), lambda i, j: (l, j, 0)),
            pl.BlockSpec((1, EXP_BLK, d), lambda i, j: (l, j, 0)),
            pl.BlockSpec((EXP_PLANES, EXP_TOK, PEER_KEYS), lambda i, j: (j, i, 0)),
            pl.BlockSpec((EXP_TOK, d), lambda i, j: (i, 0)),
            pl.BlockSpec((1, MOD_ROWS, N_MOD * d), lambda i, j: (l, 0, 0)),
        ],
        out_specs=pl.BlockSpec((EXP_TOK, d), lambda i, j: (i, 0)),
        out_shape=jax.ShapeDtypeStruct((t, d), F32),
        scratch_shapes=[pltpu.VMEM((EXP_TOK, d), F32)],
        compiler_params=_cparams(("arbitrary", "arbitrary")),
        name="peer_experts",
    )(h2, u, v, gates, x, mods)


def _final_kernel(x_ref, g_ref, o_ref):
    o_ref[...] = _rms(x_ref[...], g_ref[...])


def _final_call(x, g, row0, n_rows):
    d = x.shape[1]
    blk0 = row0 // TOK_BLK
    return pl.pallas_call(
        _final_kernel,
        grid=(n_rows // TOK_BLK,),
        in_specs=[pl.BlockSpec((TOK_BLK, d), lambda i: (blk0 + i, 0)), pl.BlockSpec((1, d), lambda i: (0, 0))],
        out_specs=pl.BlockSpec((TOK_BLK, d), lambda i: (i, 0)),
        out_shape=jax.ShapeDtypeStruct((n_rows, d), F32),
        compiler_params=_cparams(("arbitrary",)),
        name="final_norm",
    )(x, g.reshape(1, d))


def kernel(x_prompt, x_sample, c, cache_k, cache_v, c_ctx, norm1_g, norm2_g, w_ada, b_ada, w_in, conv_w,
           pool_w, pool_scale, rpb, w_out, peer_wq, peer_subkeys, peer_u, peer_v, final_g):
    n_b, seq_c, d = x_prompt.shape
    n_db, seq_l, _ = x_sample.shape
    depth = w_ada.shape[0]
    n_ctx_tok = n_b * seq_c
    assert d == D_MODEL and seq_l % GRID_W == 0 and seq_l // GRID_W >= NA_ROWS
    assert n_ctx_tok % MIX_BLK == 0 and seq_l % MIX_BLK == 0 and MIX_BLK % seq_c == 0
    assert 1 + n_db <= MOD_ROWS and (n_ctx_tok + n_db * seq_l) % EXP_TOK == 0

    x = jnp.concatenate([x_prompt.reshape(n_ctx_tok, d), x_sample.reshape(n_db * seq_l, d)], axis=0)
    cvec = jnp.concatenate([c_ctx[None, :], c, jnp.zeros((MOD_ROWS - 1 - n_db, d), F32)], axis=0)
    mods = _mods_call(cvec, w_ada, b_ada)
    w2t = _w2_call(peer_wq, peer_subkeys)
    w2t = w2t.transpose(0, 1, 3, 2, 4).reshape(depth, 2 * PEER_KEYS * PEER_HEADS, d)
    bias = _bias_call(rpb)
    past = cache_k.shape[3]
    ck = cache_k.transpose(0, 1, 3, 2, 4).reshape(n_db, depth, past, C_WIDTH)
    cv = cache_v.transpose(0, 1, 3, 2, 4).reshape(n_db, depth, past, C_WIDTH)
    w_in_b = w_in.astype(BF16)
    w_out_b = w_out.astype(BF16)
    eye = jnp.eye(B_WIDTH // POOL_GROUP_C, dtype=F32)
    pool_bd = jnp.einsum('gh,lgcd->lgchd', eye, pool_w).reshape(depth, B_WIDTH, B_WIDTH).astype(BF16)
    g1 = norm1_g.reshape(depth, 1, d)
    g2 = norm2_g.reshape(depth, 1, d)
    ps = pool_scale.reshape(depth, 1, B_WIDTH)

    k_new = jnp.zeros((n_b, depth, N_HEADS_ATTN, seq_c, HEAD_DIM), F32)
    v_new = jnp.zeros((n_b, depth, N_HEADS_ATTN, seq_c, HEAD_DIM), F32)
    for l in range(depth):
        proj = _in_call(x, mods, g1, w_in_b, l, n_ctx_tok, seq_l)
        ycp = _mix_call(proj, conv_w, pool_bd, ps, l, n_ctx_tok, seq_c, seq_l)
        y_c, k_new, v_new = _attn_ctx_call(proj, k_new, v_new, l, n_b, seq_c)
        y_l = _attn_lat_call(proj, ck, cv, bias, l, n_ctx_tok, n_db, seq_l)
        yat = jnp.concatenate([y_c, y_l], axis=0)
        x, h2 = _out_call(ycp, yat, x, mods, g2, w_out_b, l, n_ctx_tok, seq_l)
        gates = _gate_call(h2, w2t, l)
        x = _expert_call(h2, peer_u, peer_v, gates, x, mods, l, n_ctx_tok, seq_l)
    y_prompt = _final_call(x, final_g, 0, n_ctx_tok).reshape(n_b, seq_c, d)
    y_sample = _final_call(x, final_g, n_ctx_tok, n_db * seq_l).reshape(n_db, seq_l, d)
    return (y_prompt, y_sample, k_new, v_new)
```

```python
import functools

import numpy as np
import jax
import jax.numpy as jnp
from jax import lax
from jax.experimental import pallas as pl
from jax.experimental.pallas import tpu as pltpu

F32 = jnp.float32
BF16 = jnp.bfloat16

D_MODEL = 1024
DEPTH = 4
GRID_W = 64
A_WIDTH = D_MODEL // 4
B_WIDTH = D_MODEL // 4
C_WIDTH = D_MODEL // 2
N_HEADS_ATTN = 8
HEAD_DIM = C_WIDTH // N_HEADS_ATTN
POOL_GROUP_C = B_WIDTH // 4
NA_ROWS = 8
NA_COLS = 16
PEER_HEADS = 8
PEER_KEYS = 128
PEER_EXPERTS = PEER_KEYS * PEER_KEYS
PEER_TOPK = 16
PEER_HALF = 128
IN_COLS = 3 * A_WIDTH + B_WIDTH + 3 * C_WIDTH
N_MOD = 6
EPS = 1e-6
NEG_INF = -1e30
ATTN_SCALE = HEAD_DIM ** -0.5

LANES = 128
SUBLANES = 8
MOD_ROWS = 8
VMEM_LIMIT = 56 * 1024 * 1024

TOK_BLK = 512
MIX_BLK = 2048
GATE_BLK = 256
GATE_PITCH = GATE_BLK + SUBLANES
GATE_UNROLL = 64
ATTN_CTX_COLS = 256
ATTN_UNROLL = 16
EXP_TOK = 1024
EXP_BLK = 1024
EXP_PLANES = EXP_BLK // PEER_KEYS


def _cparams(sem):
    return pltpu.CompilerParams(dimension_semantics=sem, vmem_limit_bytes=VMEM_LIMIT)


def _rms(x, g):
    return x * lax.rsqrt(jnp.mean(x * x, axis=-1, keepdims=True) + EPS) * g


def _mod_row(i, blk, n_ctx_tok, dec_seq):
    start = i * blk
    return jnp.where(start < n_ctx_tok, 0, 1 + jnp.maximum(start - n_ctx_tok, 0) // dec_seq)


def _mods_kernel(c_ref, w_ref, b_ref, o_ref):
    c = c_ref[...]
    s = (c * jax.nn.sigmoid(c)).astype(BF16)
    o_ref[0] = jnp.dot(s, w_ref[0].astype(BF16), preferred_element_type=F32) + b_ref[0]


def _mods_call(cvec, w_ada, b_ada):
    depth, d, n = w_ada.shape
    tn = 1536
    return pl.pallas_call(
        _mods_kernel,
        grid=(depth, n // tn),
        in_specs=[
            pl.BlockSpec((MOD_ROWS, d), lambda l, j: (0, 0)),
            pl.BlockSpec((1, d, tn), lambda l, j: (l, 0, j)),
            pl.BlockSpec((1, 1, tn), lambda l, j: (l, 0, j)),
        ],
        out_specs=pl.BlockSpec((1, MOD_ROWS, tn), lambda l, j: (l, 0, j)),
        out_shape=jax.ShapeDtypeStruct((depth, MOD_ROWS, n), F32),
        compiler_params=_cparams(("arbitrary", "arbitrary")),
        name="mods",
    )(cvec, w_ada, b_ada.reshape(depth, 1, n))


def _w2_kernel(k_ref, wq_ref, o_ref):
    for h in range(PEER_HEADS):
        for p in range(2):
            c0 = (2 * h + p) * PEER_HALF
            o_ref[0, p, h] = lax.dot_general(
                k_ref[0, h, p], wq_ref[0, :, c0:c0 + PEER_HALF], (((1,), (1,)), ((), ())),
                precision=lax.Precision.HIGHEST, preferred_element_type=F32).astype(BF16)


def _w2_call(peer_wq, peer_subkeys):
    depth, d, nq = peer_wq.shape
    return pl.pallas_call(
        _w2_kernel,
        grid=(depth,),
        in_specs=[
            pl.BlockSpec((1, PEER_HEADS, 2, PEER_KEYS, PEER_HALF), lambda l: (l, 0, 0, 0, 0)),
            pl.BlockSpec((1, d, nq), lambda l: (l, 0, 0)),
        ],
        out_specs=pl.BlockSpec((1, 2, PEER_HEADS, PEER_KEYS, d), lambda l: (l, 0, 0, 0, 0)),
        out_shape=jax.ShapeDtypeStruct((depth, 2, PEER_HEADS, PEER_KEYS, d), BF16),
        compiler_params=_cparams(("arbitrary",)),
        name="peer_w2",
    )(peer_subkeys, peer_wq)


def _bias_kernel(rpb_ref, o_ref):
    l = pl.program_id(0)
    h = pl.program_id(1)
    n_dr = 2 * NA_ROWS - 1
    n_dc = 2 * NA_COLS - 1
    base = (l * N_HEADS_ATTN + h) * (n_dr * n_dc)
    c = lax.broadcasted_iota(jnp.int32, (GRID_W, LANES), 0)
    lane = lax.broadcasted_iota(jnp.int32, (GRID_W, LANES), 1)
    kc = lane & (GRID_W - 1)
    second = lane >= GRID_W
    diff = kc - c + (NA_COLS - 1)
    cs = jnp.clip(c - NA_COLS // 2, 0, GRID_W - NA_COLS)
    valid = (kc >= cs) & (kc < cs + NA_COLS)
    tiles = []
    for dr in range(n_dr - 1):
        t = jnp.zeros((GRID_W, LANES), F32)
        for dc in range(n_dc):
            hit = diff == dc
            t = jnp.where(hit & ~second, rpb_ref[base + dr * n_dc + dc], t)
            t = jnp.where(hit & second, rpb_ref[base + (dr + 1) * n_dc + dc], t)
        tiles.append(jnp.where(valid, t, NEG_INF))
    for v in range(NA_ROWS):
        for m in range(NA_ROWS // 2):
            o_ref[0, 0, v, :, m * LANES:(m + 1) * LANES] = tiles[2 * m - v + NA_ROWS - 1]


def _bias_call(rpb):
    depth = rpb.shape[0]
    return pl.pallas_call(
        _bias_kernel,
        grid=(depth, N_HEADS_ATTN),
        in_specs=[pl.BlockSpec(memory_space=pltpu.SMEM)],
        out_specs=pl.BlockSpec((1, 1, NA_ROWS, GRID_W, NA_ROWS * GRID_W), lambda l, h: (l, h, 0, 0, 0)),
        out_shape=jax.ShapeDtypeStruct((depth, N_HEADS_ATTN, NA_ROWS, GRID_W, NA_ROWS * GRID_W), F32),
        compiler_params=_cparams(("arbitrary", "arbitrary")),
        name="na_bias",
    )(rpb.reshape(-1))


def _in_kernel(x_ref, m_ref, g_ref, w_ref, o_ref, *, n_ctx_tok, dec_seq):
    d = D_MODEL
    row = _mod_row(pl.program_id(0), x_ref.shape[0], n_ctx_tok, dec_seq)
    sh = m_ref[0, pl.ds(row, 1), 0:d]
    sc = m_ref[0, pl.ds(row, 1), d:2 * d]
    h = _rms(x_ref[...], g_ref[0]) * (1.0 + sc) + sh
    o_ref[...] = jnp.dot(h.astype(BF16), w_ref[0], preferred_element_type=F32)


def _in_call(x, mods, g1, w_in, l, n_ctx_tok, dec_seq):
    t, d = x.shape
    n = w_in.shape[-1]
    return pl.pallas_call(
        functools.partial(_in_kernel, n_ctx_tok=n_ctx_tok, dec_seq=dec_seq),
        grid=(t // TOK_BLK,),
        in_specs=[
            pl.BlockSpec((TOK_BLK, d), lambda i: (i, 0)),
            pl.BlockSpec((1, MOD_ROWS, N_MOD * d), lambda i: (l, 0, 0)),
            pl.BlockSpec((1, 1, d), lambda i: (l, 0, 0)),
            pl.BlockSpec((1, d, n), lambda i: (l, 0, 0)),
        ],
        out_specs=pl.BlockSpec((TOK_BLK, n), lambda i: (i, 0)),
        out_shape=jax.ShapeDtypeStruct((t, n), F32),
        compiler_params=_cparams(("arbitrary",)),
        name="in_proj",
    )(x, mods, g1, w_in)


def _mix_kernel(p_ref, cw_ref, bd_ref, ps_ref, o_ref, *, n_ctx_blk, seq_c, seq_l):
    blk = p_ref.shape[0]
    lseq = jnp.where(pl.program_id(0) < n_ctx_blk, seq_c, seq_l)
    rows = lax.broadcasted_iota(jnp.int32, (blk, A_WIDTH), 0)
    lane = lax.broadcasted_iota(jnp.int32, (blk, A_WIDTH), 1)
    tpos = rows & (lseq - 1)

    def shift_dn(x, m):
        return jnp.where(tpos >= m, pltpu.roll(x, m, 0), 0.0)

    def shift_up(x, m):
        return jnp.where(tpos + m < lseq, pltpu.roll(x, blk - m, 0), 0.0)

    a_in = p_ref[:, 0:A_WIDTH]
    a_b = p_ref[:, A_WIDTH:2 * A_WIDTH]
    a_c = p_ref[:, 2 * A_WIDTH:3 * A_WIDTH]
    u = a_c * a_in
    cw = cw_ref[0]
    y = cw[0:1] * shift_dn(u, 1) + cw[1:2] * u + cw[2:3] * shift_up(u, 1)
    o_ref[:, 0:A_WIDTH] = (a_b * y).astype(o_ref.dtype)

    p = p_ref[:, 3 * A_WIDTH:3 * A_WIDTH + B_WIDTH]
    f1 = p
    f2 = f1 + shift_up(f1, 1)
    f4 = f2 + shift_up(f2, 2)
    f8 = f4 + shift_up(f4, 4)
    b1 = shift_dn(p, 1)
    b2 = b1 + shift_dn(b1, 1)
    b4 = b2 + shift_dn(b2, 2)
    b8 = b4 + shift_dn(b4, 4)
    grp = jnp.right_shift(lane, POOL_GROUP_C.bit_length() - 1)
    wsum = jnp.where(grp == 0, b1 + f1, jnp.where(grp == 1, b2 + f2, jnp.where(grp == 2, b4 + f4, b8 + f8)))
    half = jnp.left_shift(1, grp)
    lo = jnp.maximum(tpos - half, 0)
    hi = jnp.minimum(tpos + half - 1, lseq - 1)
    cnt = (hi - lo + 1).astype(F32)
    dlt = wsum / cnt - p
    o_ref[:, A_WIDTH:A_WIDTH + B_WIDTH] = (
        jnp.dot(dlt.astype(BF16), bd_ref[0], preferred_element_type=F32) * ps_ref[0]).astype(o_ref.dtype)


def _mix_call(proj, conv_w, pool_bd, pool_scale, l, n_ctx_tok, seq_c, seq_l):
    t = proj.shape[0]
    return pl.pallas_call(
        functools.partial(_mix_kernel, n_ctx_blk=n_ctx_tok // MIX_BLK, seq_c=seq_c, seq_l=seq_l),
        grid=(t // MIX_BLK,),
        in_specs=[
            pl.BlockSpec((MIX_BLK, 3 * A_WIDTH + B_WIDTH), lambda i: (i, 0)),
            pl.BlockSpec((1, 3, A_WIDTH), lambda i: (l, 0, 0)),
            pl.BlockSpec((1, B_WIDTH, B_WIDTH), lambda i: (l, 0, 0)),
            pl.BlockSpec((1, 1, B_WIDTH), lambda i: (l, 0, 0)),
        ],
        out_specs=pl.BlockSpec((MIX_BLK, A_WIDTH + B_WIDTH), lambda i: (i, 0)),
        out_shape=jax.ShapeDtypeStruct((t, A_WIDTH + B_WIDTH), BF16),
        compiler_params=_cparams(("arbitrary",)),
        name="conv_pool",
    )(proj, conv_w, pool_bd, pool_scale)


def _attn_ctx_kernel(q_ref, k_ref, v_ref, kin_ref, vin_ref, y_ref, ko_ref, vo_ref):
    del kin_ref, vin_ref
    q = q_ref[...]
    k = k_ref[...]
    v = v_ref[...]
    head = jnp.right_shift(lax.broadcasted_iota(jnp.int32, q.shape, 1), HEAD_DIM.bit_length() - 1)
    kb = k.astype(BF16)
    vb = v.astype(BF16)
    y = jnp.zeros(q.shape, F32)
    for h in range(q.shape[1] // HEAD_DIM):
        qh = jnp.where(head == h, q, 0.0).astype(BF16)
        s = lax.dot_general(qh, kb, (((1,), (1,)), ((), ())), preferred_element_type=F32) * ATTN_SCALE
        e = jnp.exp(s - jnp.max(s, axis=-1, keepdims=True))
        pr = e / jnp.sum(e, axis=-1, keepdims=True)
        y = jnp.where(head == h, jnp.dot(pr.astype(BF16), vb, preferred_element_type=F32), y)
        ko_ref[0, 0, h] = k[:, h * HEAD_DIM:(h + 1) * HEAD_DIM]
        vo_ref[0, 0, h] = v[:, h * HEAD_DIM:(h + 1) * HEAD_DIM]
    y_ref[...] = y.astype(y_ref.dtype)


def _attn_ctx_call(proj, k_new, v_new, l, n_batch, seq):
    width = ATTN_CTX_COLS
    col0 = (3 * A_WIDTH + B_WIDTH) // width
    ncol = C_WIDTH // width
    kv_shape = jax.ShapeDtypeStruct(k_new.shape, F32)
    kv_spec = pl.BlockSpec((1, 1, width // HEAD_DIM, seq, HEAD_DIM), lambda b, hp: (b, l, hp, 0, 0))
    return pl.pallas_call(
        _attn_ctx_kernel,
        grid=(n_batch, ncol),
        in_specs=[
            pl.BlockSpec((seq, width), lambda b, hp: (b, col0 + hp)),
            pl.BlockSpec((seq, width), lambda b, hp: (b, col0 + ncol + hp)),
            pl.BlockSpec((seq, width), lambda b, hp: (b, col0 + 2 * ncol + hp)),
            pl.BlockSpec(memory_space=pl.ANY),
            pl.BlockSpec(memory_space=pl.ANY),
        ],
        out_specs=[pl.BlockSpec((seq, width), lambda b, hp: (b, hp)), kv_spec, kv_spec],
        out_shape=[jax.ShapeDtypeStruct((n_batch * seq, C_WIDTH), BF16), kv_shape, kv_shape],
        input_output_aliases={3: 1, 4: 2},
        compiler_params=_cparams(("arbitrary", "arbitrary")),
        name="attn_ctx",
    )(proj, proj, proj, k_new, v_new)


def _attn_lat_kernel(q_ref, k_ref, v_ref, ck_ref, cv_ref, bv_ref, y_ref, s_ref, e_ref, oc_ref, zi_ref):
    n = q_ref.shape[0]
    rows = n // GRID_W
    nrow = min(NA_ROWS, rows)
    nwin = nrow * GRID_W
    dn = (((1,), (1,)), ((), ()))
    ck = ck_ref[0, 0].astype(BF16)
    cv = cv_ref[0, 0].astype(BF16)
    lane_all = lax.broadcasted_iota(jnp.int32, (n, LANES), 1)
    lane = lax.broadcasted_iota(jnp.int32, (GRID_W, LANES), 1)

    for h in range(2):
        second = h == 1

        def scores(r, carry):
            rs = jnp.clip(r - nrow // 2, 0, rows - nrow)
            qrow = pl.ds(pl.multiple_of(r * GRID_W, GRID_W), GRID_W)
            qh = jnp.where((lane >= HEAD_DIM) == second, q_ref[qrow, :], 0.0).astype(BF16)
            kw = k_ref[pl.ds(pl.multiple_of(rs * GRID_W, GRID_W), nwin), :].astype(BF16)
            s_ref[qrow, :] = (lax.dot_general(qh, kw, dn, preferred_element_type=F32) * ATTN_SCALE
                              + bv_ref[0, h, r - rs])
            return carry

        lax.fori_loop(0, rows, scores, 0, unroll=ATTN_UNROLL)

        qh_all = jnp.where((lane_all >= HEAD_DIM) == second, q_ref[...], 0.0).astype(BF16)
        s_c = lax.dot_general(qh_all, ck, dn, preferred_element_type=F32) * ATTN_SCALE
        s_w = s_ref[...]
        mx = jnp.maximum(jnp.max(s_w, axis=-1, keepdims=True), jnp.max(s_c, axis=-1, keepdims=True))
        e_w = jnp.exp(s_w - mx)
        e_c = jnp.exp(s_c - mx)
        z = jnp.sum(e_w, axis=-1, keepdims=True) + jnp.sum(e_c, axis=-1, keepdims=True)
        e_ref[...] = e_w.astype(BF16)
        oc_ref[...] = jnp.dot(e_c.astype(BF16), cv, preferred_element_type=F32)
        zi_ref[...] = jnp.broadcast_to(1.0 / z, (n, LANES))

        def values(r, carry):
            rs = jnp.clip(r - nrow // 2, 0, rows - nrow)
            qrow = pl.ds(pl.multiple_of(r * GRID_W, GRID_W), GRID_W)
            vw = v_ref[pl.ds(pl.multiple_of(rs * GRID_W, GRID_W), nwin), :].astype(BF16)
            o = (jnp.dot(e_ref[qrow, :], vw, preferred_element_type=F32) + oc_ref[qrow, :]) * zi_ref[qrow, :]
            if second:
                o = jnp.where(lane < HEAD_DIM, y_ref[qrow, :].astype(F32), o)
            y_ref[qrow, :] = o.astype(y_ref.dtype)
            return carry

        lax.fori_loop(0, rows, values, 0, unroll=ATTN_UNROLL)


def _attn_lat_call(proj, ck, cv, bias, l, n_ctx_tok, n_batch, seq):
    col0 = (3 * A_WIDTH + B_WIDTH) // LANES
    ncol = C_WIDTH // LANES
    row0 = n_ctx_tok // seq
    past = ck.shape[2]
    return pl.pallas_call(
        _attn_lat_kernel,
        grid=(n_batch, ncol),
        in_specs=[
            pl.BlockSpec((seq, LANES), lambda b, hp: (row0 + b, col0 + hp)),
            pl.BlockSpec((seq, LANES), lambda b, hp: (row0 + b, col0 + ncol + hp)),
            pl.BlockSpec((seq, LANES), lambda b, hp: (row0 + b, col0 + 2 * ncol + hp)),
            pl.BlockSpec((1, 1, past, LANES), lambda b, hp: (b, l, 0, hp)),
            pl.BlockSpec((1, 1, past, LANES), lambda b, hp: (b, l, 0, hp)),
            pl.BlockSpec((1, 2, NA_ROWS, GRID_W, NA_ROWS * GRID_W), lambda b, hp: (l, hp, 0, 0, 0)),
        ],
        out_specs=pl.BlockSpec((seq, LANES), lambda b, hp: (b, hp)),
        out_shape=jax.ShapeDtypeStruct((n_batch * seq, C_WIDTH), BF16),
        scratch_shapes=[pltpu.VMEM((seq, NA_ROWS * GRID_W), F32), pltpu.VMEM((seq, NA_ROWS * GRID_W), BF16),
                        pltpu.VMEM((seq, LANES), F32), pltpu.VMEM((seq, LANES), F32)],
        compiler_params=_cparams(("arbitrary", "arbitrary")),
        name="attn_lat",
    )(proj, proj, proj, ck, cv, bias)


def _out_kernel(ycp_ref, yat_ref, x_ref, m_ref, g_ref, w_ref, xo_ref, h_ref, *, n_ctx_tok, dec_seq):
    d = D_MODEL
    half = A_WIDTH + B_WIDTH
    row = _mod_row(pl.program_id(0), x_ref.shape[0], n_ctx_tok, dec_seq)
    y = (jnp.dot(ycp_ref[...], w_ref[0, 0:half], preferred_element_type=F32)
         + jnp.dot(yat_ref[...], w_ref[0, half:2 * half], preferred_element_type=F32))
    x = x_ref[...] + m_ref[0, pl.ds(row, 1), 2 * d:3 * d] * y
    xo_ref[...] = x
    sh = m_ref[0, pl.ds(row, 1), 3 * d:4 * d]
    sc = m_ref[0, pl.ds(row, 1), 4 * d:5 * d]
    h_ref[...] = (_rms(x, g_ref[0]) * (1.0 + sc) + sh).astype(BF16)


def _out_call(ycp, yat, x, mods, g2, w_out, l, n_ctx_tok, dec_seq):
    t, d = x.shape
    half = ycp.shape[1]
    return pl.pallas_call(
        functools.partial(_out_kernel, n_ctx_tok=n_ctx_tok, dec_seq=dec_seq),
        grid=(t // TOK_BLK,),
        in_specs=[
            pl.BlockSpec((TOK_BLK, half), lambda i: (i, 0)),
            pl.BlockSpec((TOK_BLK, half), lambda i: (i, 0)),
            pl.BlockSpec((TOK_BLK, d), lambda i: (i, 0)),
            pl.BlockSpec((1, MOD_ROWS, N_MOD * d), lambda i: (l, 0, 0)),
            pl.BlockSpec((1, 1, d), lambda i: (l, 0, 0)),
            pl.BlockSpec((1, d, d), lambda i: (l, 0, 0)),
        ],
        out_specs=[pl.BlockSpec((TOK_BLK, d), lambda i: (i, 0)), pl.BlockSpec((TOK_BLK, d), lambda i: (i, 0))],
        out_shape=[jax.ShapeDtypeStruct((t, d), F32), jax.ShapeDtypeStruct((t, d), BF16)],
        compiler_params=_cparams(("arbitrary",)),
        name="out_proj",
    )(ycp, yat, x, mods, g2, w_out)


def _oem_sort_pairs(n):
    pairs = []

    def merge(lo, m, r):
        step = r * 2
        if step < m:
            merge(lo, m, step)
            merge(lo + r, m, step)
            pairs.extend((i, i + r) for i in range(lo + r, lo + m - r, step))
        else:
            pairs.append((lo, lo + r))

    def sort(lo, m):
        if m > 1:
            sort(lo, m // 2)
            sort(lo + m // 2, m // 2)
            merge(lo, m, 1)

    sort(0, n)
    return pairs


_SORT16 = _oem_sort_pairs(PEER_TOPK)
_CAND = [(a, b) for a in range(PEER_TOPK) for b in range(PEER_TOPK) if (a + 1) * (b + 1) <= PEER_TOPK]


def _cmp_exchange(vals, idxs, i, j):
    first = vals[i] >= vals[j]
    vals[i], vals[j] = jnp.maximum(vals[i], vals[j]), jnp.minimum(vals[i], vals[j])
    idxs[i], idxs[j] = jnp.where(first, idxs[i], idxs[j]), jnp.where(first, idxs[j], idxs[i])


def _top16(load):
    k = PEER_TOPK
    best = None
    for g in range(PEER_KEYS // k):
        vals = [load(g * k + i) for i in range(k)]
        idxs = [jnp.full(vals[0].shape, float(g * k + i), F32) for i in range(k)]
        for i, j in _SORT16:
            _cmp_exchange(vals, idxs, i, j)
        if best is not None:
            bv, bi = best
            for i in range(k):
                first = bv[i] >= vals[k - 1 - i]
                bi[i] = jnp.where(first, bi[i], idxs[k - 1 - i])
                bv[i] = jnp.maximum(bv[i], vals[k - 1 - i])
            stride = k // 2
            while stride >= 1:
                for i in range(k):
                    if i & stride == 0:
                        _cmp_exchange(bv, bi, i, i + stride)
                stride //= 2
            vals, idxs = bv, bi
        best = (vals, idxs)
    return best


def _gate_kernel(h_ref, w2_ref, o_ref, sc_ref, i1_ref, i2_ref, g_ref, gs_ref):
    nk = PEER_KEYS
    k = PEER_TOPK
    sc_ref[...] = lax.dot_general(w2_ref[0], h_ref[...], (((1,), (1,)), ((), ())), preferred_element_type=F32)

    def topk_group(sb):
        lanes = slice(sb * LANES, (sb + 1) * LANES)
        tops = [_top16(lambda key, p=p, lanes=lanes: sc_ref[pl.ds((p * nk + key) * PEER_HEADS, PEER_HEADS), lanes])
                for p in range(2)]
        (v1, x1), (v2, x2) = tops
        cand = [v1[a] + v2[b] for a, b in _CAND]
        slot_v, slot_1, slot_2 = [], [], []
        for it in range(k):
            live = [n for n, (a, b) in enumerate(_CAND) if (a + 1) * (b + 1) <= it + 1]
            mx = functools.reduce(jnp.maximum, [cand[n] for n in live])
            found = jnp.zeros(mx.shape, jnp.bool_)
            s1 = jnp.zeros(mx.shape, F32)
            s2 = jnp.zeros(mx.shape, F32)
            for n in live:
                a, b = _CAND[n]
                eq = cand[n] == mx
                sel = eq & ~found
                found = found | eq
                s1 = jnp.where(sel, x1[a], s1)
                s2 = jnp.where(sel, x2[b], s2)
                cand[n] = jnp.where(sel, -jnp.inf, cand[n])
            slot_v.append(mx)
            slot_1.append(s1)
            slot_2.append(s2)
        e = [jnp.exp(v - slot_v[0]) for v in slot_v]
        z = functools.reduce(jnp.add, e)
        gates = [x / z for x in e]
        i1_ref[lanes, :] = jnp.concatenate(slot_1, axis=0).T
        i2_ref[lanes, :] = jnp.concatenate(slot_2, axis=0).T
        g_ref[lanes, :] = jnp.concatenate(gates, axis=0).T
    pk = 2 * SUBLANES
    key_iota = lax.broadcasted_iota(jnp.int32, (nk // pk, pk, nk), 0) * pk + lax.broadcasted_iota(
        jnp.int32, (nk // pk, pk, nk), 1)
    key_iota = key_iota.astype(F32).astype(BF16)
    one = jnp.ones((), BF16)
    zero = jnp.zeros((), BF16)

    def row(ref, t):
        return jnp.broadcast_to(ref[pl.ds(t, 1), :], (pk, nk)).astype(BF16)[None]

    def per_token(t, carry):
        lt = jnp.where(key_iota == row(i1_ref, t), one, zero).reshape(nk, nk)
        rt = jnp.where(key_iota == row(i2_ref, t), row(g_ref, t), zero).reshape(nk, nk)
        gt = lax.dot_general(lt, rt, (((1,), (1,)), ((), ())), preferred_element_type=F32)
        for r in range(nk // SUBLANES):
            gs_ref[pl.ds(r * SUBLANES * GATE_PITCH + t, SUBLANES, stride=GATE_PITCH), :] = (
                gt[r * SUBLANES:(r + 1) * SUBLANES, :])
        return carry

    for sb in range(GATE_BLK // LANES):
        topk_group(sb)
    lax.fori_loop(0, GATE_BLK, per_token, 0, unroll=GATE_UNROLL)
    for i1 in range(nk):
        o_ref[i1] = gs_ref[pl.ds(i1 * GATE_PITCH, GATE_BLK), :].astype(BF16)


def _gate_call(h2, w2t, l):
    t, d = h2.shape
    nrow = w2t.shape[1]
    return pl.pallas_call(
        _gate_kernel,
        grid=(t // GATE_BLK,),
        in_specs=[
            pl.BlockSpec((GATE_BLK, d), lambda i: (i, 0)),
            pl.BlockSpec((1, nrow, d), lambda i: (l, 0, 0)),
        ],
        out_specs=pl.BlockSpec((PEER_KEYS, GATE_BLK, PEER_KEYS), lambda i: (0, i, 0)),
        out_shape=jax.ShapeDtypeStruct((PEER_KEYS, t, PEER_KEYS), BF16),
        scratch_shapes=[
            pltpu.VMEM((nrow, GATE_BLK), F32),
            pltpu.VMEM((GATE_BLK, PEER_HEADS * PEER_TOPK), F32),
            pltpu.VMEM((GATE_BLK, PEER_HEADS * PEER_TOPK), F32),
            pltpu.VMEM((GATE_BLK, PEER_HEADS * PEER_TOPK), F32),
            pltpu.VMEM((PEER_KEYS * GATE_PITCH, PEER_KEYS), F32),
        ],
        compiler_params=_cparams(("arbitrary",)),
        name="peer_gates",
    )(h2, w2t)


def _gelu(x):
    return 0.5 * x * (1.0 + jnp.tanh(0.7978845608028654 * (x + 0.044715 * (x * x * x))))


def _expert_kernel(h_ref, u_ref, v_ref, g_ref, x_ref, m_ref, o_ref, acc_ref, *, n_ctx_tok, dec_seq):
    d = D_MODEL
    j = pl.program_id(1)

    @pl.when(j == 0)
    def _():
        acc_ref[...] = jnp.zeros_like(acc_ref)

    a = lax.dot_general(h_ref[...], u_ref[0].astype(BF16), (((1,), (1,)), ((), ())), preferred_element_type=F32)
    hid = jnp.concatenate(
        [(_gelu(a[:, q * PEER_KEYS:(q + 1) * PEER_KEYS]) * g_ref[q].astype(F32)).astype(BF16)
         for q in range(EXP_PLANES)], axis=1)
    acc_ref[...] += jnp.dot(hid, v_ref[0].astype(BF16), preferred_element_type=F32)

    @pl.when(j == pl.num_programs(1) - 1)
    def _():
        row = _mod_row(pl.program_id(0), x_ref.shape[0], n_ctx_tok, dec_seq)
        o_ref[...] = x_ref[...] + m_ref[0, pl.ds(row, 1), 5 * d:6 * d] * acc_ref[...]


def _expert_call(h2, u, v, gates, x, mods, l, n_ctx_tok, dec_seq):
    t, d = x.shape
    ne = u.shape[1]
    return pl.pallas_call(
        functools.partial(_expert_kernel, n_ctx_tok=n_ctx_tok, dec_seq=dec_seq),
        grid=(t // EXP_TOK, ne // EXP_BLK),
        in_specs=[
            pl.BlockSpec((EXP_TOK, d), lambda i, j: (i, 0)),
            pl.BlockSpec((1, EXP_BLK, d), lambda i, j: (l, j, 0)),
            pl.BlockSpec((1, EXP_BLK, d), lambda i, j: (l, j, 0)),
            pl.BlockSpec((EXP_PLANES, EXP_TOK, PEER_KEYS), lambda i, j: (j, i, 0)),
            pl.BlockSpec((EXP_TOK, d), lambda i, j: (i, 0)),
            pl.BlockSpec((1, MOD_ROWS, N_MOD * d), lambda i, j: (l, 0, 0)),
        ],
        out_specs=pl.BlockSpec((EXP_TOK, d), lambda i, j: (i, 0)),
        out_shape=jax.ShapeDtypeStruct((t, d), F32),
        scratch_shapes=[pltpu.VMEM((EXP_TOK, d), F32)],
        compiler_params=_cparams(("arbitrary", "arbitrary")),
        name="peer_experts",
    )(h2, u, v, gates, x, mods)


def _final_kernel(x_ref, g_ref, o_ref):
    o_ref[...] = _rms(x_ref[...], g_ref[...])


def _final_call(x, g, row0, n_rows):
    d = x.shape[1]
    blk0 = row0 // TOK_BLK
    return pl.pallas_call(
        _final_kernel,
        grid=(n_rows // TOK_BLK,),
        in_specs=[pl.BlockSpec((TOK_BLK, d), lambda i: (blk0 + i, 0)), pl.BlockSpec((1, d), lambda i: (0, 0))],
        out_specs=pl.BlockSpec((TOK_BLK, d), lambda i: (i, 0)),
        out_shape=jax.ShapeDtypeStruct((n_rows, d), F32),
        compiler_params=_cparams(("arbitrary",)),
        name="final_norm",
    )(x, g.reshape(1, d))


def kernel(x_prompt, x_sample, c, cache_k, cache_v, c_ctx, norm1_g, norm2_g, w_ada, b_ada, w_in, conv_w,
           pool_w, pool_scale, rpb, w_out, peer_wq, peer_subkeys, peer_u, peer_v, final_g):
    n_b, seq_c, d = x_prompt.shape
    n_db, seq_l, _ = x_sample.shape
    depth = w_ada.shape[0]
    n_ctx_tok = n_b * seq_c
    assert d == D_MODEL and seq_l % GRID_W == 0 and seq_l // GRID_W >= NA_ROWS
    assert n_ctx_tok % MIX_BLK == 0 and seq_l % MIX_BLK == 0 and MIX_BLK % seq_c == 0
    assert 1 + n_db <= MOD_ROWS and (n_ctx_tok + n_db * seq_l) % EXP_TOK == 0

    x = jnp.concatenate([x_prompt.reshape(n_ctx_tok, d), x_sample.reshape(n_db * seq_l, d)], axis=0)
    cvec = jnp.concatenate([c_ctx[None, :], c, jnp.zeros((MOD_ROWS - 1 - n_db, d), F32)], axis=0)
    mods = _mods_call(cvec, w_ada, b_ada)
    w2t = _w2_call(peer_wq, peer_subkeys)
    w2t = w2t.transpose(0, 1, 3, 2, 4).reshape(depth, 2 * PEER_KEYS * PEER_HEADS, d)
    bias = _bias_call(rpb)
    past = cache_k.shape[3]
    ck = cache_k.transpose(0, 1, 3, 2, 4).reshape(n_db, depth, past, C_WIDTH)
    cv = cache_v.transpose(0, 1, 3, 2, 4).reshape(n_db, depth, past, C_WIDTH)
    w_in_b = w_in.astype(BF16)
    w_out_b = w_out.astype(BF16)
    eye = jnp.eye(B_WIDTH // POOL_GROUP_C, dtype=F32)
    pool_bd = jnp.einsum('gh,lgcd->lgchd', eye, pool_w).reshape(depth, B_WIDTH, B_WIDTH).astype(BF16)
    g1 = norm1_g.reshape(depth, 1, d)
    g2 = norm2_g.reshape(depth, 1, d)
    ps = pool_scale.reshape(depth, 1, B_WIDTH)

    k_new = jnp.zeros((n_b, depth, N_HEADS_ATTN, seq_c, HEAD_DIM), F32)
    v_new = jnp.zeros((n_b, depth, N_HEADS_ATTN, seq_c, HEAD_DIM), F32)
    for l in range(depth):
        proj = _in_call(x, mods, g1, w_in_b, l, n_ctx_tok, seq_l)
        ycp = _mix_call(proj, conv_w, pool_bd, ps, l, n_ctx_tok, seq_c, seq_l)
        y_c, k_new, v_new = _attn_ctx_call(proj, k_new, v_new, l, n_b, seq_c)
        y_l = _attn_lat_call(proj, ck, cv, bias, l, n_ctx_tok, n_db, seq_l)
        yat = jnp.concatenate([y_c, y_l], axis=0)
        x, h2 = _out_call(ycp, yat, x, mods, g2, w_out_b, l, n_ctx_tok, seq_l)
        gates = _gate_call(h2, w2t, l)
        x = _expert_call(h2, peer_u, peer_v, gates, x, mods, l, n_ctx_tok, seq_l)
    y_prompt = _final_call(x, final_g, 0, n_ctx_tok).reshape(n_b, seq_c, d)
    y_sample = _final_call(x, final_g, n_ctx_tok, n_db * seq_l).reshape(n_db, seq_l, d)
    return (y_prompt, y_sample, k_new, v_new)
```

```python
import functools

import numpy as np
import jax
import jax.numpy as jnp
from jax import lax
from jax.experimental import pallas as pl
from jax.experimental.pallas import tpu as pltpu

F32 = jnp.float32
BF16 = jnp.bfloat16

D_MODEL = 1024
DEPTH = 4
GRID_W = 64
A_WIDTH = D_MODEL // 4
B_WIDTH = D_MODEL // 4
C_WIDTH = D_MODEL // 2
N_HEADS_ATTN = 8
HEAD_DIM = C_WIDTH // N_HEADS_ATTN
POOL_GROUP_C = B_WIDTH // 4
NA_ROWS = 8
NA_COLS = 16
PEER_HEADS = 8
PEER_KEYS = 128
PEER_EXPERTS = PEER_KEYS * PEER_KEYS
PEER_TOPK = 16
PEER_HALF = 128
IN_COLS = 3 * A_WIDTH + B_WIDTH + 3 * C_WIDTH
N_MOD = 6
EPS = 1e-6
NEG_INF = -1e30
ATTN_SCALE = HEAD_DIM ** -0.5

LANES = 128
SUBLANES = 8
MOD_ROWS = 8
VMEM_LIMIT = 56 * 1024 * 1024

TOK_BLK = 512
MIX_BLK = 2048
GATE_BLK = 256
GATE_PITCH = GATE_BLK + SUBLANES
GATE_UNROLL = 128
ATTN_CTX_COLS = 256
ATTN_UNROLL = 16
EXP_TOK = 1024
EXP_BLK = 1024
EXP_PLANES = EXP_BLK // PEER_KEYS


def _cparams(sem):
    return pltpu.CompilerParams(dimension_semantics=sem, vmem_limit_bytes=VMEM_LIMIT)


def _rms(x, g):
    return x * lax.rsqrt(jnp.mean(x * x, axis=-1, keepdims=True) + EPS) * g


def _mod_row(i, blk, n_ctx_tok, dec_seq):
    start = i * blk
    return jnp.where(start < n_ctx_tok, 0, 1 + jnp.maximum(start - n_ctx_tok, 0) // dec_seq)


def _mods_kernel(c_ref, w_ref, b_ref, o_ref):
    c = c_ref[...]
    s = (c * jax.nn.sigmoid(c)).astype(BF16)
    o_ref[0] = jnp.dot(s, w_ref[0].astype(BF16), preferred_element_type=F32) + b_ref[0]


def _mods_call(cvec, w_ada, b_ada):
    depth, d, n = w_ada.shape
    tn = 1536
    return pl.pallas_call(
        _mods_kernel,
        grid=(depth, n // tn),
        in_specs=[
            pl.BlockSpec((MOD_ROWS, d), lambda l, j: (0, 0)),
            pl.BlockSpec((1, d, tn), lambda l, j: (l, 0, j)),
            pl.BlockSpec((1, 1, tn), lambda l, j: (l, 0, j)),
        ],
        out_specs=pl.BlockSpec((1, MOD_ROWS, tn), lambda l, j: (l, 0, j)),
        out_shape=jax.ShapeDtypeStruct((depth, MOD_ROWS, n), F32),
        compiler_params=_cparams(("arbitrary", "arbitrary")),
        name="mods",
    )(cvec, w_ada, b_ada.reshape(depth, 1, n))


def _w2_kernel(k_ref, wq_ref, o_ref):
    for h in range(PEER_HEADS):
        for p in range(2):
            c0 = (2 * h + p) * PEER_HALF
            o_ref[0, p, h] = lax.dot_general(
                k_ref[0, h, p], wq_ref[0, :, c0:c0 + PEER_HALF], (((1,), (1,)), ((), ())),
                precision=lax.Precision.HIGHEST, preferred_element_type=F32).astype(BF16)


def _w2_call(peer_wq, peer_subkeys):
    depth, d, nq = peer_wq.shape
    return pl.pallas_call(
        _w2_kernel,
        grid=(depth,),
        in_specs=[
            pl.BlockSpec((1, PEER_HEADS, 2, PEER_KEYS, PEER_HALF), lambda l: (l, 0, 0, 0, 0)),
            pl.BlockSpec((1, d, nq), lambda l: (l, 0, 0)),
        ],
        out_specs=pl.BlockSpec((1, 2, PEER_HEADS, PEER_KEYS, d), lambda l: (l, 0, 0, 0, 0)),
        out_shape=jax.ShapeDtypeStruct((depth, 2, PEER_HEADS, PEER_KEYS, d), BF16),
        compiler_params=_cparams(("arbitrary",)),
        name="peer_w2",
    )(peer_subkeys, peer_wq)


def _bias_kernel(rpb_ref, o_ref):
    l = pl.program_id(0)
    h = pl.program_id(1)
    n_dr = 2 * NA_ROWS - 1
    n_dc = 2 * NA_COLS - 1
    base = (l * N_HEADS_ATTN + h) * (n_dr * n_dc)
    c = lax.broadcasted_iota(jnp.int32, (GRID_W, LANES), 0)
    lane = lax.broadcasted_iota(jnp.int32, (GRID_W, LANES), 1)
    kc = lane & (GRID_W - 1)
    second = lane >= GRID_W
    diff = kc - c + (NA_COLS - 1)
    cs = jnp.clip(c - NA_COLS // 2, 0, GRID_W - NA_COLS)
    valid = (kc >= cs) & (kc < cs + NA_COLS)
    tiles = []
    for dr in range(n_dr - 1):
        t = jnp.zeros((GRID_W, LANES), F32)
        for dc in range(n_dc):
            hit = diff == dc
            t = jnp.where(hit & ~second, rpb_ref[base + dr * n_dc + dc], t)
            t = jnp.where(hit & second, rpb_ref[base + (dr + 1) * n_dc + dc], t)
        tiles.append(jnp.where(valid, t, NEG_INF))
    for v in range(NA_ROWS):
        for m in range(NA_ROWS // 2):
            o_ref[0, 0, v, :, m * LANES:(m + 1) * LANES] = tiles[2 * m - v + NA_ROWS - 1]


def _bias_call(rpb):
    depth = rpb.shape[0]
    return pl.pallas_call(
        _bias_kernel,
        grid=(depth, N_HEADS_ATTN),
        in_specs=[pl.BlockSpec(memory_space=pltpu.SMEM)],
        out_specs=pl.BlockSpec((1, 1, NA_ROWS, GRID_W, NA_ROWS * GRID_W), lambda l, h: (l, h, 0, 0, 0)),
        out_shape=jax.ShapeDtypeStruct((depth, N_HEADS_ATTN, NA_ROWS, GRID_W, NA_ROWS * GRID_W), F32),
        compiler_params=_cparams(("arbitrary", "arbitrary")),
        name="na_bias",
    )(rpb.reshape(-1))


def _in_kernel(x_ref, m_ref, g_ref, w_ref, o_ref, *, n_ctx_tok, dec_seq):
    d = D_MODEL
    row = _mod_row(pl.program_id(0), x_ref.shape[0], n_ctx_tok, dec_seq)
    sh = m_ref[0, pl.ds(row, 1), 0:d]
    sc = m_ref[0, pl.ds(row, 1), d:2 * d]
    h = _rms(x_ref[...], g_ref[0]) * (1.0 + sc) + sh
    o_ref[...] = jnp.dot(h.astype(BF16), w_ref[0], preferred_element_type=F32)


def _in_call(x, mods, g1, w_in, l, n_ctx_tok, dec_seq):
    t, d = x.shape
    n = w_in.shape[-1]
    return pl.pallas_call(
        functools.partial(_in_kernel, n_ctx_tok=n_ctx_tok, dec_seq=dec_seq),
        grid=(t // TOK_BLK,),
        in_specs=[
            pl.BlockSpec((TOK_BLK, d), lambda i: (i, 0)),
            pl.BlockSpec((1, MOD_ROWS, N_MOD * d), lambda i: (l, 0, 0)),
            pl.BlockSpec((1, 1, d), lambda i: (l, 0, 0)),
            pl.BlockSpec((1, d, n), lambda i: (l, 0, 0)),
        ],
        out_specs=pl.BlockSpec((TOK_BLK, n), lambda i: (i, 0)),
        out_shape=jax.ShapeDtypeStruct((t, n), F32),
        compiler_params=_cparams(("arbitrary",)),
        name="in_proj",
    )(x, mods, g1, w_in)


def _mix_kernel(p_ref, cw_ref, bd_ref, ps_ref, o_ref, *, n_ctx_blk, seq_c, seq_l):
    blk = p_ref.shape[0]
    lseq = jnp.where(pl.program_id(0) < n_ctx_blk, seq_c, seq_l)
    rows = lax.broadcasted_iota(jnp.int32, (blk, A_WIDTH), 0)
    lane = lax.broadcasted_iota(jnp.int32, (blk, A_WIDTH), 1)
    tpos = rows & (lseq - 1)

    def shift_dn(x, m):
        return jnp.where(tpos >= m, pltpu.roll(x, m, 0), 0.0)

    def shift_up(x, m):
        return jnp.where(tpos + m < lseq, pltpu.roll(x, blk - m, 0), 0.0)

    a_in = p_ref[:, 0:A_WIDTH]
    a_b = p_ref[:, A_WIDTH:2 * A_WIDTH]
    a_c = p_ref[:, 2 * A_WIDTH:3 * A_WIDTH]
    u = a_c * a_in
    cw = cw_ref[0]
    y = cw[0:1] * shift_dn(u, 1) + cw[1:2] * u + cw[2:3] * shift_up(u, 1)
    o_ref[:, 0:A_WIDTH] = (a_b * y).astype(o_ref.dtype)

    p = p_ref[:, 3 * A_WIDTH:3 * A_WIDTH + B_WIDTH]
    f1 = p
    f2 = f1 + shift_up(f1, 1)
    f4 = f2 + shift_up(f2, 2)
    f8 = f4 + shift_up(f4, 4)
    b1 = shift_dn(p, 1)
    b2 = b1 + shift_dn(b1, 1)
    b4 = b2 + shift_dn(b2, 2)
    b8 = b4 + shift_dn(b4, 4)
    grp = jnp.right_shift(lane, POOL_GROUP_C.bit_length() - 1)
    wsum = jnp.where(grp == 0, b1 + f1, jnp.where(grp == 1, b2 + f2, jnp.where(grp == 2, b4 + f4, b8 + f8)))
    half = jnp.left_shift(1, grp)
    lo = jnp.maximum(tpos - half, 0)
    hi = jnp.minimum(tpos + half - 1, lseq - 1)
    cnt = (hi - lo + 1).astype(F32)
    dlt = wsum / cnt - p
    o_ref[:, A_WIDTH:A_WIDTH + B_WIDTH] = (
        jnp.dot(dlt.astype(BF16), bd_ref[0], preferred_element_type=F32) * ps_ref[0]).astype(o_ref.dtype)


def _mix_call(proj, conv_w, pool_bd, pool_scale, l, n_ctx_tok, seq_c, seq_l):
    t = proj.shape[0]
    return pl.pallas_call(
        functools.partial(_mix_kernel, n_ctx_blk=n_ctx_tok // MIX_BLK, seq_c=seq_c, seq_l=seq_l),
        grid=(t // MIX_BLK,),
        in_specs=[
            pl.BlockSpec((MIX_BLK, 3 * A_WIDTH + B_WIDTH), lambda i: (i, 0)),
            pl.BlockSpec((1, 3, A_WIDTH), lambda i: (l, 0, 0)),
            pl.BlockSpec((1, B_WIDTH, B_WIDTH), lambda i: (l, 0, 0)),
            pl.BlockSpec((1, 1, B_WIDTH), lambda i: (l, 0, 0)),
        ],
        out_specs=pl.BlockSpec((MIX_BLK, A_WIDTH + B_WIDTH), lambda i: (i, 0)),
        out_shape=jax.ShapeDtypeStruct((t, A_WIDTH + B_WIDTH), BF16),
        compiler_params=_cparams(("arbitrary",)),
        name="conv_pool",
    )(proj, conv_w, pool_bd, pool_scale)


def _attn_ctx_kernel(q_ref, k_ref, v_ref, kin_ref, vin_ref, y_ref, ko_ref, vo_ref):
    del kin_ref, vin_ref
    q = q_ref[...]
    k = k_ref[...]
    v = v_ref[...]
    head = jnp.right_shift(lax.broadcasted_iota(jnp.int32, q.shape, 1), HEAD_DIM.bit_length() - 1)
    kb = k.astype(BF16)
    vb = v.astype(BF16)
    y = jnp.zeros(q.shape, F32)
    for h in range(q.shape[1] // HEAD_DIM):
        qh = jnp.where(head == h, q, 0.0).astype(BF16)
        s = lax.dot_general(qh, kb, (((1,), (1,)), ((), ())), preferred_element_type=F32) * ATTN_SCALE
        e = jnp.exp(s - jnp.max(s, axis=-1, keepdims=True))
        pr = e / jnp.sum(e, axis=-1, keepdims=True)
        y = jnp.where(head == h, jnp.dot(pr.astype(BF16), vb, preferred_element_type=F32), y)
        ko_ref[0, 0, h] = k[:, h * HEAD_DIM:(h + 1) * HEAD_DIM]
        vo_ref[0, 0, h] = v[:, h * HEAD_DIM:(h + 1) * HEAD_DIM]
    y_ref[...] = y.astype(y_ref.dtype)


def _attn_ctx_call(proj, k_new, v_new, l, n_batch, seq):
    width = ATTN_CTX_COLS
    col0 = (3 * A_WIDTH + B_WIDTH) // width
    ncol = C_WIDTH // width
    kv_shape = jax.ShapeDtypeStruct(k_new.shape, F32)
    kv_spec = pl.BlockSpec((1, 1, width // HEAD_DIM, seq, HEAD_DIM), lambda b, hp: (b, l, hp, 0, 0))
    return pl.pallas_call(
        _attn_ctx_kernel,
        grid=(n_batch, ncol),
        in_specs=[
            pl.BlockSpec((seq, width), lambda b, hp: (b, col0 + hp)),
            pl.BlockSpec((seq, width), lambda b, hp: (b, col0 + ncol + hp)),
            pl.BlockSpec((seq, width), lambda b, hp: (b, col0 + 2 * ncol + hp)),
            pl.BlockSpec(memory_space=pl.ANY),
            pl.BlockSpec(memory_space=pl.ANY),
        ],
        out_specs=[pl.BlockSpec((seq, width), lambda b, hp: (b, hp)), kv_spec, kv_spec],
        out_shape=[jax.ShapeDtypeStruct((n_batch * seq, C_WIDTH), BF16), kv_shape, kv_shape],
        input_output_aliases={3: 1, 4: 2},
        compiler_params=_cparams(("arbitrary", "arbitrary")),
        name="attn_ctx",
    )(proj, proj, proj, k_new, v_new)


def _attn_lat_kernel(q_ref, k_ref, v_ref, ck_ref, cv_ref, bv_ref, y_ref, s_ref, e_ref, oc_ref, zi_ref):
    n = q_ref.shape[0]
    rows = n // GRID_W
    nrow = min(NA_ROWS, rows)
    nwin = nrow * GRID_W
    dn = (((1,), (1,)), ((), ()))
    ck = ck_ref[0, 0].astype(BF16)
    cv = cv_ref[0, 0].astype(BF16)
    lane_all = lax.broadcasted_iota(jnp.int32, (n, LANES), 1)
    lane = lax.broadcasted_iota(jnp.int32, (GRID_W, LANES), 1)

    for h in range(2):
        second = h == 1

        def scores(r, carry):
            rs = jnp.clip(r - nrow // 2, 0, rows - nrow)
            qrow = pl.ds(pl.multiple_of(r * GRID_W, GRID_W), GRID_W)
            qh = jnp.where((lane >= HEAD_DIM) == second, q_ref[qrow, :], 0.0).astype(BF16)
            kw = k_ref[pl.ds(pl.multiple_of(rs * GRID_W, GRID_W), nwin), :].astype(BF16)
            s_ref[qrow, :] = (lax.dot_general(qh, kw, dn, preferred_element_type=F32) * ATTN_SCALE
                              + bv_ref[0, h, r - rs])
            return carry

        lax.fori_loop(0, rows, scores, 0, unroll=ATTN_UNROLL)

        qh_all = jnp.where((lane_all >= HEAD_DIM) == second, q_ref[...], 0.0).astype(BF16)
        s_c = lax.dot_general(qh_all, ck, dn, preferred_element_type=F32) * ATTN_SCALE
        s_w = s_ref[...]
        mx = jnp.maximum(jnp.max(s_w, axis=-1, keepdims=True), jnp.max(s_c, axis=-1, keepdims=True))
        e_w = jnp.exp(s_w - mx)
        e_c = jnp.exp(s_c - mx)
        z = jnp.sum(e_w, axis=-1, keepdims=True) + jnp.sum(e_c, axis=-1, keepdims=True)
        e_ref[...] = e_w.astype(BF16)
        oc_ref[...] = jnp.dot(e_c.astype(BF16), cv, preferred_element_type=F32)
        zi_ref[...] = jnp.broadcast_to(1.0 / z, (n, LANES))

        def values(r, carry):
            rs = jnp.clip(r - nrow // 2, 0, rows - nrow)
            qrow = pl.ds(pl.multiple_of(r * GRID_W, GRID_W), GRID_W)
            vw = v_ref[pl.ds(pl.multiple_of(rs * GRID_W, GRID_W), nwin), :].astype(BF16)
            o = (jnp.dot(e_ref[qrow, :], vw, preferred_element_type=F32) + oc_ref[qrow, :]) * zi_ref[qrow, :]
            if second:
                o = jnp.where(lane < HEAD_DIM, y_ref[qrow, :].astype(F32), o)
            y_ref[qrow, :] = o.astype(y_ref.dtype)
            return carry

        lax.fori_loop(0, rows, values, 0, unroll=ATTN_UNROLL)


def _attn_lat_call(proj, ck, cv, bias, l, n_ctx_tok, n_batch, seq):
    col0 = (3 * A_WIDTH + B_WIDTH) // LANES
    ncol = C_WIDTH // LANES
    row0 = n_ctx_tok // seq
    past = ck.shape[2]
    return pl.pallas_call(
        _attn_lat_kernel,
        grid=(n_batch, ncol),
        in_specs=[
            pl.BlockSpec((seq, LANES), lambda b, hp: (row0 + b, col0 + hp)),
            pl.BlockSpec((seq, LANES), lambda b, hp: (row0 + b, col0 + ncol + hp)),
            pl.BlockSpec((seq, LANES), lambda b, hp: (row0 + b, col0 + 2 * ncol + hp)),
            pl.BlockSpec((1, 1, past, LANES), lambda b, hp: (b, l, 0, hp)),
            pl.BlockSpec((1, 1, past, LANES), lambda b, hp: (b, l, 0, hp)),
            pl.BlockSpec((1, 2, NA_ROWS, GRID_W, NA_ROWS * GRID_W), lambda b, hp: (l, hp, 0, 0, 0)),
        ],
        out_specs=pl.BlockSpec((seq, LANES), lambda b, hp: (b, hp)),
        out_shape=jax.ShapeDtypeStruct((n_batch * seq, C_WIDTH), BF16),
        scratch_shapes=[pltpu.VMEM((seq, NA_ROWS * GRID_W), F32), pltpu.VMEM((seq, NA_ROWS * GRID_W), BF16),
                        pltpu.VMEM((seq, LANES), F32), pltpu.VMEM((seq, LANES), F32)],
        compiler_params=_cparams(("arbitrary", "arbitrary")),
        name="attn_lat",
    )(proj, proj, proj, ck, cv, bias)


def _out_kernel(ycp_ref, yat_ref, x_ref, m_ref, g_ref, w_ref, xo_ref, h_ref, *, n_ctx_tok, dec_seq):
    d = D_MODEL
    half = A_WIDTH + B_WIDTH
    row = _mod_row(pl.program_id(0), x_ref.shape[0], n_ctx_tok, dec_seq)
    y = (jnp.dot(ycp_ref[...], w_ref[0, 0:half], preferred_element_type=F32)
         + jnp.dot(yat_ref[...], w_ref[0, half:2 * half], preferred_element_type=F32))
    x = x_ref[...] + m_ref[0, pl.ds(row, 1), 2 * d:3 * d] * y
    xo_ref[...] = x
    sh = m_ref[0, pl.ds(row, 1), 3 * d:4 * d]
    sc = m_ref[0, pl.ds(row, 1), 4 * d:5 * d]
    h_ref[...] = (_rms(x, g_ref[0]) * (1.0 + sc) + sh).astype(BF16)


def _out_call(ycp, yat, x, mods, g2, w_out, l, n_ctx_tok, dec_seq):
    t, d = x.shape
    half = ycp.shape[1]
    return pl.pallas_call(
        functools.partial(_out_kernel, n_ctx_tok=n_ctx_tok, dec_seq=dec_seq),
        grid=(t // TOK_BLK,),
        in_specs=[
            pl.BlockSpec((TOK_BLK, half), lambda i: (i, 0)),
            pl.BlockSpec((TOK_BLK, half), lambda i: (i, 0)),
            pl.BlockSpec((TOK_BLK, d), lambda i: (i, 0)),
            pl.BlockSpec((1, MOD_ROWS, N_MOD * d), lambda i: (l, 0, 0)),
            pl.BlockSpec((1, 1, d), lambda i: (l, 0, 0)),
            pl.BlockSpec((1, d, d), lambda i: (l, 0, 0)),
        ],
        out_specs=[pl.BlockSpec((TOK_BLK, d), lambda i: (i, 0)), pl.BlockSpec((TOK_BLK, d), lambda i: (i, 0))],
        out_shape=[jax.ShapeDtypeStruct((t, d), F32), jax.ShapeDtypeStruct((t, d), BF16)],
        compiler_params=_cparams(("arbitrary",)),
        name="out_proj",
    )(ycp, yat, x, mods, g2, w_out)


def _oem_sort_pairs(n):
    pairs = []

    def merge(lo, m, r):
        step = r * 2
        if step < m:
            merge(lo, m, step)
            merge(lo + r, m, step)
            pairs.extend((i, i + r) for i in range(lo + r, lo + m - r, step))
        else:
            pairs.append((lo, lo + r))

    def sort(lo, m):
        if m > 1:
            sort(lo, m // 2)
            sort(lo + m // 2, m // 2)
            merge(lo, m, 1)

    sort(0, n)
    return pairs


_SORT16 = _oem_sort_pairs(PEER_TOPK)
_CAND = [(a, b) for a in range(PEER_TOPK) for b in range(PEER_TOPK) if (a + 1) * (b + 1) <= PEER_TOPK]


def _cmp_exchange(vals, idxs, i, j):
    first = vals[i] >= vals[j]
    vals[i], vals[j] = jnp.maximum(vals[i], vals[j]), jnp.minimum(vals[i], vals[j])
    idxs[i], idxs[j] = jnp.where(first, idxs[i], idxs[j]), jnp.where(first, idxs[j], idxs[i])


def _top16(load):
    k = PEER_TOPK
    best = None
    for g in range(PEER_KEYS // k):
        vals = [load(g * k + i) for i in range(k)]
        idxs = [jnp.full(vals[0].shape, float(g * k + i), F32) for i in range(k)]
        for i, j in _SORT16:
            _cmp_exchange(vals, idxs, i, j)
        if best is not None:
            bv, bi = best
            for i in range(k):
                first = bv[i] >= vals[k - 1 - i]
                bi[i] = jnp.where(first, bi[i], idxs[k - 1 - i])
                bv[i] = jnp.maximum(bv[i], vals[k - 1 - i])
            stride = k // 2
            while stride >= 1:
                for i in range(k):
                    if i & stride == 0:
                        _cmp_exchange(bv, bi, i, i + stride)
                stride //= 2
            vals, idxs = bv, bi
        best = (vals, idxs)
    return best


def _gate_kernel(h_ref, w2_ref, o_ref, sc_ref, i1_ref, i2_ref, g_ref, gs_ref):
    nk = PEER_KEYS
    k = PEER_TOPK
    sc_ref[...] = lax.dot_general(w2_ref[0], h_ref[...], (((1,), (1,)), ((), ())), preferred_element_type=F32)

    def topk_group(sb):
        lanes = slice(sb * LANES, (sb + 1) * LANES)
        tops = [_top16(lambda key, p=p, lanes=lanes: sc_ref[pl.ds((p * nk + key) * PEER_HEADS, PEER_HEADS), lanes])
                for p in range(2)]
        (v1, x1), (v2, x2) = tops
        cand = [v1[a] + v2[b] for a, b in _CAND]
        slot_v, slot_1, slot_2 = [], [], []
        for it in range(k):
            live = [n for n, (a, b) in enumerate(_CAND) if (a + 1) * (b + 1) <= it + 1]
            mx = functools.reduce(jnp.maximum, [cand[n] for n in live])
            found = jnp.zeros(mx.shape, jnp.bool_)
            s1 = jnp.zeros(mx.shape, F32)
            s2 = jnp.zeros(mx.shape, F32)
            for n in live:
                a, b = _CAND[n]
                eq = cand[n] == mx
                sel = eq & ~found
                found = found | eq
                s1 = jnp.where(sel, x1[a], s1)
                s2 = jnp.where(sel, x2[b], s2)
                cand[n] = jnp.where(sel, -jnp.inf, cand[n])
            slot_v.append(mx)
            slot_1.append(s1)
            slot_2.append(s2)
        e = [jnp.exp(v - slot_v[0]) for v in slot_v]
        z = functools.reduce(jnp.add, e)
        gates = [x / z for x in e]
        i1_ref[lanes, :] = jnp.concatenate(slot_1, axis=0).T
        i2_ref[lanes, :] = jnp.concatenate(slot_2, axis=0).T
        g_ref[lanes, :] = jnp.concatenate(gates, axis=0).T
    pk = 2 * SUBLANES
    key_iota = lax.broadcasted_iota(jnp.int32, (nk // pk, pk, nk), 0) * pk + lax.broadcasted_iota(
        jnp.int32, (nk // pk, pk, nk), 1)
    key_iota = key_iota.astype(F32).astype(BF16)
    one = jnp.ones((), BF16)
    zero = jnp.zeros((), BF16)

    def row(ref, t):
        return jnp.broadcast_to(ref[pl.ds(t, 1), :], (pk, nk)).astype(BF16)[None]

    def per_token(t, carry):
        lt = jnp.where(key_iota == row(i1_ref, t), one, zero).reshape(nk, nk)
        rt = jnp.where(key_iota == row(i2_ref, t), row(g_ref, t), zero).reshape(nk, nk)
        gt = lax.dot_general(lt, rt, (((1,), (1,)), ((), ())), preferred_element_type=F32)
        for r in range(nk // SUBLANES):
            gs_ref[pl.ds(r * SUBLANES * GATE_PITCH + t, SUBLANES, stride=GATE_PITCH), :] = (
                gt[r * SUBLANES:(r + 1) * SUBLANES, :])
        return carry

    for sb in range(GATE_BLK // LANES):
        topk_group(sb)
    lax.fori_loop(0, GATE_BLK, per_token, 0, unroll=GATE_UNROLL)
    for i1 in range(nk):
        o_ref[i1] = gs_ref[pl.ds(i1 * GATE_PITCH, GATE_BLK), :].astype(BF16)


def _gate_call(h2, w2t, l):
    t, d = h2.shape
    nrow = w2t.shape[1]
    return pl.pallas_call(
        _gate_kernel,
        grid=(t // GATE_BLK,),
        in_specs=[
            pl.BlockSpec((GATE_BLK, d), lambda i: (i, 0)),
            pl.BlockSpec((1, nrow, d), lambda i: (l, 0, 0)),
        ],
        out_specs=pl.BlockSpec((PEER_KEYS, GATE_BLK, PEER_KEYS), lambda i: (0, i, 0)),
        out_shape=jax.ShapeDtypeStruct((PEER_KEYS, t, PEER_KEYS), BF16),
        scratch_shapes=[
            pltpu.VMEM((nrow, GATE_BLK), F32),
            pltpu.VMEM((GATE_BLK, PEER_HEADS * PEER_TOPK), F32),
            pltpu.VMEM((GATE_BLK, PEER_HEADS * PEER_TOPK), F32),
            pltpu.VMEM((GATE_BLK, PEER_HEADS * PEER_TOPK), F32),
            pltpu.VMEM((PEER_KEYS * GATE_PITCH, PEER_KEYS), F32),
        ],
        compiler_params=_cparams(("arbitrary",)),
        name="peer_gates",
    )(h2, w2t)


def _gelu(x):
    return 0.5 * x * (1.0 + jnp.tanh(0.7978845608028654 * (x + 0.044715 * (x * x * x))))


def _expert_kernel(h_ref, u_ref, v_ref, g_ref, x_ref, m_ref, o_ref, acc_ref, *, n_ctx_tok, dec_seq):
    d = D_MODEL
    j = pl.program_id(1)

    @pl.when(j == 0)
    def _():
        acc_ref[...] = jnp.zeros_like(acc_ref)

    a = lax.dot_general(h_ref[...], u_ref[0].astype(BF16), (((1,), (1,)), ((), ())), preferred_element_type=F32)
    hid = jnp.concatenate(
        [(_gelu(a[:, q * PEER_KEYS:(q + 1) * PEER_KEYS]) * g_ref[q].astype(F32)).astype(BF16)
         for q in range(EXP_PLANES)], axis=1)
    acc_ref[...] += jnp.dot(hid, v_ref[0].astype(BF16), preferred_element_type=F32)

    @pl.when(j == pl.num_programs(1) - 1)
    def _():
        row = _mod_row(pl.program_id(0), x_ref.shape[0], n_ctx_tok, dec_seq)
        o_ref[...] = x_ref[...] + m_ref[0, pl.ds(row, 1), 5 * d:6 * d] * acc_ref[...]


def _expert_call(h2, u, v, gates, x, mods, l, n_ctx_tok, dec_seq):
    t, d = x.shape
    ne = u.shape[1]
    return pl.pallas_call(
        functools.partial(_expert_kernel, n_ctx_tok=n_ctx_tok, dec_seq=dec_seq),
        grid=(t // EXP_TOK, ne // EXP_BLK),
        in_specs=[
            pl.BlockSpec((EXP_TOK, d), lambda i, j: (i, 0)),
            pl.BlockSpec((1, EXP_BLK, d), lambda i, j: (l, j, 0)),
            pl.BlockSpec((1, EXP_BLK, d), lambda i, j: (l, j, 0)),
            pl.BlockSpec((EXP_PLANES, EXP_TOK, PEER_KEYS), lambda i, j: (j, i, 0)),
            pl.BlockSpec((EXP_TOK, d), lambda i, j: (i, 0)),
            pl.BlockSpec((1, MOD_ROWS, N_MOD * d), lambda i, j: (l, 0, 0)),
        ],
        out_specs=pl.BlockSpec((EXP_TOK, d), lambda i, j: (i, 0)),
        out_shape=jax.ShapeDtypeStruct((t, d), F32),
        scratch_shapes=[pltpu.VMEM((EXP_TOK, d), F32)],
        compiler_params=_cparams(("arbitrary", "arbitrary")),
        name="peer_experts",
    )(h2, u, v, gates, x, mods)


def _final_kernel(x_ref, g_ref, o_ref):
    o_ref[...] = _rms(x_ref[...], g_ref[...])


def _final_call(x, g, row0, n_rows):
    d = x.shape[1]
    blk0 = row0 // TOK_BLK
    return pl.pallas_call(
        _final_kernel,
        grid=(n_rows // TOK_BLK,),
        in_specs=[pl.BlockSpec((TOK_BLK, d), lambda i: (blk0 + i, 0)), pl.BlockSpec((1, d), lambda i: (0, 0))],
        out_specs=pl.BlockSpec((TOK_BLK, d), lambda i: (i, 0)),
        out_shape=jax.ShapeDtypeStruct((n_rows, d), F32),
        compiler_params=_cparams(("arbitrary",)),
        name="final_norm",
    )(x, g.reshape(1, d))


def kernel(x_prompt, x_sample, c, cache_k, cache_v, c_ctx, norm1_g, norm2_g, w_ada, b_ada, w_in, conv_w,
           pool_w, pool_scale, rpb, w_out, peer_wq, peer_subkeys, peer_u, peer_v, final_g):
    n_b, seq_c, d = x_prompt.shape
    n_db, seq_l, _ = x_sample.shape
    depth = w_ada.shape[0]
    n_ctx_tok = n_b * seq_c
    assert d == D_MODEL and seq_l % GRID_W == 0 and seq_l // GRID_W >= NA_ROWS
    assert n_ctx_tok % MIX_BLK == 0 and seq_l % MIX_BLK == 0 and MIX_BLK % seq_c == 0
    assert 1 + n_db <= MOD_ROWS and (n_ctx_tok + n_db * seq_l) % EXP_TOK == 0

    x = jnp.concatenate([x_prompt.reshape(n_ctx_tok, d), x_sample.reshape(n_db * seq_l, d)], axis=0)
    cvec = jnp.concatenate([c_ctx[None, :], c, jnp.zeros((MOD_ROWS - 1 - n_db, d), F32)], axis=0)
    mods = _mods_call(cvec, w_ada, b_ada)
    w2t = _w2_call(peer_wq, peer_subkeys)
    w2t = w2t.transpose(0, 1, 3, 2, 4).reshape(depth, 2 * PEER_KEYS * PEER_HEADS, d)
    bias = _bias_call(rpb)
    past = cache_k.shape[3]
    ck = cache_k.transpose(0, 1, 3, 2, 4).reshape(n_db, depth, past, C_WIDTH)
    cv = cache_v.transpose(0, 1, 3, 2, 4).reshape(n_db, depth, past, C_WIDTH)
    w_in_b = w_in.astype(BF16)
    w_out_b = w_out.astype(BF16)
    eye = jnp.eye(B_WIDTH // POOL_GROUP_C, dtype=F32)
    pool_bd = jnp.einsum('gh,lgcd->lgchd', eye, pool_w).reshape(depth, B_WIDTH, B_WIDTH).astype(BF16)
    g1 = norm1_g.reshape(depth, 1, d)
    g2 = norm2_g.reshape(depth, 1, d)
    ps = pool_scale.reshape(depth, 1, B_WIDTH)

    k_new = jnp.zeros((n_b, depth, N_HEADS_ATTN, seq_c, HEAD_DIM), F32)
    v_new = jnp.zeros((n_b, depth, N_HEADS_ATTN, seq_c, HEAD_DIM), F32)
    for l in range(depth):
        proj = _in_call(x, mods, g1, w_in_b, l, n_ctx_tok, seq_l)
        ycp = _mix_call(proj, conv_w, pool_bd, ps, l, n_ctx_tok, seq_c, seq_l)
        y_c, k_new, v_new = _attn_ctx_call(proj, k_new, v_new, l, n_b, seq_c)
        y_l = _attn_lat_call(proj, ck, cv, bias, l, n_ctx_tok, n_db, seq_l)
        yat = jnp.concatenate([y_c, y_l], axis=0)
        x, h2 = _out_call(ycp, yat, x, mods, g2, w_out_b, l, n_ctx_tok, seq_l)
        gates = _gate_call(h2, w2t, l)
        x = _expert_call(h2, peer_u, peer_v, gates, x, mods, l, n_ctx_tok, seq_l)
    y_prompt = _final_call(x, final_g, 0, n_ctx_tok).reshape(n_b, seq_c, d)
    y_sample = _final_call(x, final_g, n_ctx_tok, n_db * seq_l).reshape(n_db, seq_l, d)
    return (y_prompt, y_sample, k_new, v_new)
```
